```python
import math
import jax, jax.numpy as jnp
from jax import lax
import numpy as np

D_MODEL = 1024
BATCH = 16
SEQ = 2048
DEPTH = 2
DEC_BATCH = 8
DEC_SEQ = 32
PAST_LEN = 1024

CHUNK = 64
CONV_W = 4
EPS = 1e-6
N_AB_LAYERS = (DEPTH + 1) // 2
N_C_LAYERS = DEPTH // 2
SSD_HEADS = 16
SSD_HEAD_DIM = 64
SSD_INNER = SSD_HEADS * SSD_HEAD_DIM
SSD_GROUPS = 2
SSD_STATE = 128
SSD_CONV_DIM = SSD_INNER + 2 * SSD_GROUPS * SSD_STATE
GDN_K_HEADS = 4
GDN_V_HEADS = 8
GDN_K_DIM = 128
GDN_V_DIM = 128
GDN_CONV_DIM = 2 * GDN_K_HEADS * GDN_K_DIM + GDN_V_HEADS * GDN_V_DIM
AB_IN_DIM = SSD_INNER + SSD_CONV_DIM + SSD_HEADS + GDN_CONV_DIM + GDN_V_HEADS * GDN_V_DIM + 2 * GDN_V_HEADS
AB_MIX_DIM = SSD_INNER + GDN_V_HEADS * GDN_V_DIM
HGRN_HEADS = 8
HGRN_K_DIM = 128
HGRN_V_DIM = 128
HGRN_IN_DIM = HGRN_HEADS * (2 * HGRN_K_DIM + 2 * HGRN_V_DIM)
HGRN_BLOCK = 16
MOE_GROUPS = 4
MOE_EXPERTS_PER_GROUP = 8
MOE_EXPERTS = MOE_GROUPS * MOE_EXPERTS_PER_GROUP
MOE_TOP_K = 2
MOE_D_EXPERT = 512
MOE_ROWS = 128

kernel_name = 'hybrid_ssd_gdn_hgrn2_hmoe_stream_step'


def rms_norm(x, w):
    xf = x.astype(jnp.float32)
    y = xf * lax.rsqrt(jnp.mean(xf * xf, axis=-1, keepdims=True) + EPS)
    return (y * w.astype(jnp.float32)).astype(x.dtype)


def l2_normalize(x):
    xf = x.astype(jnp.float32)
    return (xf * lax.rsqrt(jnp.sum(xf * xf, axis=-1, keepdims=True) + EPS)).astype(x.dtype)


def split_cols(a, sizes):
    return jnp.split(a, np.cumsum(sizes)[:-1].tolist(), axis=-1)


def causal_mask(l, strict):
    i = jnp.arange(l)
    return (i[:, None] > i[None, :]) if strict else (i[:, None] >= i[None, :])


def causal_dwconv(x, prev, w, b=None):
    xp = jnp.concatenate([prev.astype(x.dtype), x], axis=1)
    y = lax.conv_general_dilated(xp, w.astype(x.dtype)[:, None, :], (1,), 'VALID',
                                 dimension_numbers=('NWC', 'WIO', 'NWC'),
                                 feature_group_count=x.shape[-1])
    if b is not None:
        y = y + b.astype(x.dtype)
    return y, xp[:, xp.shape[1] - (CONV_W - 1):]


def scan_blocks(step, state0, xs, block):
    bsz, l = xs[0].shape[:2]
    nb = l // block
    to_blocks = lambda a: jnp.swapaxes(a.reshape(bsz, nb, block, *a.shape[2:]), 0, 1)
    state, ys = lax.scan(step, state0, tuple(to_blocks(a) for a in xs))
    return jnp.swapaxes(ys, 0, 1).reshape(bsz, l, *ys.shape[3:]), state


def ssd_block(h, inp):
    x, dt, a, bm, cm = inp
    l, n_h = x.shape[1], x.shape[2]
    rep = n_h // bm.shape[2]
    bh = jnp.repeat(bm, rep, axis=2)
    ch = jnp.repeat(cm, rep, axis=2)
    cum = jnp.cumsum(a.astype(jnp.float32), axis=1)
    decay = jnp.exp(jnp.where(causal_mask(l, False)[None, :, :, None],
                              cum[:, :, None, :] - cum[:, None, :, :], -jnp.inf))
    w_ts = jnp.einsum('bthn,bshn->btsh', ch, bh) * decay * dt[:, None, :, :]
    y = jnp.einsum('btsh,bshp->bthp', w_ts, x)
    y = y + jnp.einsum('bthn,bhpn->bthp', ch, h) * jnp.exp(cum)[..., None]
    tail = jnp.exp(cum[:, -1:, :] - cum) * dt
    h_new = h * jnp.exp(cum[:, -1])[:, :, None, None] + jnp.einsum('bshn,bsh,bshp->bhpn', bh, tail, x)
    return h_new.astype(h.dtype), y


def gdn_block(s_mat, inp):
    q, k, v, beta, g = inp
    l = q.shape[1]
    gam = jnp.cumsum(g.astype(jnp.float32), axis=1)
    diff = gam[:, :, None, :] - gam[:, None, :, :]
    dec_incl = jnp.exp(jnp.where(causal_mask(l, False)[None, :, :, None], diff, -jnp.inf))
    dec_strict = jnp.where(causal_mask(l, True)[None, :, :, None], dec_incl, 0.0)
    a_mat = (beta[:, :, None, :] * dec_strict * jnp.einsum('bthk,bshk->btsh', k, k)).transpose(0, 3, 1, 2)
    eg = jnp.exp(gam)
    rhs = beta[..., None] * (v - eg[..., None] * jnp.einsum('bthk,bhkv->bthv', k, s_mat))
    u = lax.linalg.triangular_solve(a_mat.astype(jnp.float32) + jnp.eye(l, dtype=jnp.float32),
                                    rhs.transpose(0, 2, 1, 3).astype(jnp.float32),
                                    left_side=True, lower=True, unit_diagonal=True)
    qk = jnp.einsum('bthk,bshk->bhts', q, k) * dec_incl.transpose(0, 3, 1, 2)
    o = jnp.einsum('bhts,bhsv->bthv', qk, u) + eg[..., None] * jnp.einsum('bthk,bhkv->bthv', q, s_mat)
    tail = jnp.exp(gam[:, -1:, :] - gam)
    s_new = s_mat * jnp.exp(gam[:, -1])[:, :, None, None] + jnp.einsum('bshk,bsh,bhsv->bhkv', k, tail, u)
    return s_new.astype(s_mat.dtype), o


def hgrn_block(s_mat, inp):
    q, k, v, logf = inp
    l = q.shape[1]
    gam = jnp.cumsum(logf.astype(jnp.float32), axis=1)
    dec = jnp.exp(jnp.where(causal_mask(l, False)[None, :, :, None, None],
                            gam[:, :, None] - gam[:, None], -jnp.inf))
    att = jnp.einsum('bthk,btshk,bshk->bhts', q, dec, k)
    o = jnp.einsum('bhts,bshv->bthv', att, v) + jnp.einsum('bthk,bhkv->bthv', q * jnp.exp(gam), s_mat)
    s_new = s_mat * jnp.exp(gam[:, -1])[..., None] + jnp.einsum('bshk,bshv->bhkv', k * jnp.exp(gam[:, -1:] - gam), v)
    return s_new.astype(s_mat.dtype), o


def ab_mixer(u, s_ssd0, conv_ssd0, s_gdn0, conv_gdn0, w, j, block):
    bsz, l, _ = u.shape
    z_s, xbc, dt_raw, qkv, z_g, b_raw, a_raw = split_cols(
        u @ w['w_in_ab'][j],
        [SSD_INNER, SSD_CONV_DIM, SSD_HEADS, GDN_CONV_DIM, GDN_V_HEADS * GDN_V_DIM, GDN_V_HEADS, GDN_V_HEADS])
    xbc, conv_ssd = causal_dwconv(xbc, conv_ssd0, w['ssd_conv_w'][j], w['ssd_conv_b'][j])
    xs, bm, cm = split_cols(jax.nn.silu(xbc), [SSD_INNER, SSD_GROUPS * SSD_STATE, SSD_GROUPS * SSD_STATE])
    xs = xs.reshape(bsz, l, SSD_HEADS, SSD_HEAD_DIM)
    bm = bm.reshape(bsz, l, SSD_GROUPS, SSD_STATE)
    cm = cm.reshape(bsz, l, SSD_GROUPS, SSD_STATE)
    dt = jax.nn.softplus(dt_raw.astype(jnp.float32) + w['ssd_dt_bias'][j].astype(jnp.float32))
    a = -jnp.exp(w['ssd_a_log'][j].astype(jnp.float32)) * dt
    y, s_ssd = scan_blocks(ssd_block, s_ssd0, (xs, dt, a, bm, cm), block)
    y = (y + w['ssd_d'][j][:, None] * xs).reshape(bsz, l, SSD_INNER) * jax.nn.silu(z_s)
    y = rms_norm(y.reshape(bsz, l, SSD_GROUPS, SSD_INNER // SSD_GROUPS),
                 w['ssd_norm'][j].reshape(SSD_GROUPS, -1)).reshape(bsz, l, SSD_INNER)
    qkv, conv_gdn = causal_dwconv(qkv, conv_gdn0, w['gdn_conv_w'][j])
    q, k, v = split_cols(jax.nn.silu(qkv), [GDN_K_HEADS * GDN_K_DIM, GDN_K_HEADS * GDN_K_DIM, GDN_V_HEADS * GDN_V_DIM])
    rep = GDN_V_HEADS // GDN_K_HEADS
    q = jnp.repeat(l2_normalize(q.reshape(bsz, l, GDN_K_HEADS, GDN_K_DIM)), rep, axis=2) * (GDN_K_DIM ** -0.5)
    k = jnp.repeat(l2_normalize(k.reshape(bsz, l, GDN_K_HEADS, GDN_K_DIM)), rep, axis=2)
    v = v.reshape(bsz, l, GDN_V_HEADS, GDN_V_DIM)
    beta = jax.nn.sigmoid(b_raw.astype(jnp.float32))
    g = -jnp.exp(w['gdn_a_log'][j].astype(jnp.float32)) * jax.nn.softplus(
        a_raw.astype(jnp.float32) + w['gdn_dt_bias'][j].astype(jnp.float32))
    o, s_gdn = scan_blocks(gdn_block, s_gdn0, (q, k, v, beta, g), block)
    o = rms_norm(o, w['gdn_norm'][j]) * jax.nn.silu(z_g.reshape(bsz, l, GDN_V_HEADS, GDN_V_DIM))
    mix = jnp.concatenate([y, o.reshape(bsz, l, -1)], axis=-1) @ w['w_out_ab'][j]
    return mix.astype(u.dtype), s_ssd, conv_ssd, s_gdn, conv_gdn


def hgrn_mixer(u, s0, lb, w, j, block):
    bsz, l, _ = u.shape
    shp_k = (bsz, l, HGRN_HEADS, HGRN_K_DIM)
    shp_v = (bsz, l, HGRN_HEADS, HGRN_V_DIM)
    q, fz, i, gz = split_cols(u @ w['w_in_c'][j], [HGRN_HEADS * HGRN_K_DIM] * 2 + [HGRN_HEADS * HGRN_V_DIM] * 2)
    q = jax.nn.silu(q).reshape(shp_k)
    lb = lb.reshape(HGRN_HEADS, HGRN_K_DIM)
    f = lb + (1.0 - lb) * jax.nn.sigmoid(fz.astype(jnp.float32).reshape(shp_k))
    o, s_new = scan_blocks(hgrn_block, s0, (q, 1.0 - f, i.reshape(shp_v), jnp.log(f)), block)
    o = rms_norm(o, w['hgrn_norm'][j]) * jax.nn.silu(gz.reshape(shp_v))
    mix = o.reshape(bsz, l, -1) @ w['w_out_c'][j]
    return mix.astype(u.dtype), s_new


def hier_moe(x, w_group, w_router, w_gate, w_up, w_down):
    bsz, l, d = x.shape
    xt = x.reshape(bsz * l, d)
    n_tok = xt.shape[0]
    xf = xt.astype(jnp.float32)
    group_p = jax.nn.softmax(xf @ w_group.astype(jnp.float32), axis=-1)
    g_sel = jnp.argmax(group_p, axis=-1)
    g_prob = jnp.take_along_axis(group_p, g_sel[:, None], axis=-1)
    e_logits = (xf @ w_router.astype(jnp.float32)).reshape(n_tok, MOE_GROUPS, MOE_EXPERTS_PER_GROUP)
    e_logits = jnp.take_along_axis(e_logits, g_sel[:, None, None], axis=1)[:, 0]
    top_v, top_i = lax.top_k(e_logits, MOE_TOP_K)
    gates = jax.nn.softmax(top_v, axis=-1) * g_prob
    expert = g_sel[:, None] * MOE_EXPERTS_PER_GROUP + top_i
    n_as = n_tok * MOE_TOP_K
    flat_e = expert.reshape(-1)
    order = jnp.argsort(flat_e)
    e_sorted = flat_e[order]
    tok = order // MOE_TOP_K
    counts = jnp.bincount(flat_e, length=MOE_EXPERTS)
    padded = (counts + MOE_ROWS - 1) // MOE_ROWS * MOE_ROWS
    pad_end = jnp.cumsum(padded)
    dest = (pad_end - padded)[e_sorted] + jnp.arange(n_as) - (jnp.cumsum(counts) - counts)[e_sorted]
    n_blocks = (n_as + MOE_EXPERTS * (MOE_ROWS - 1) + MOE_ROWS - 1) // MOE_ROWS
    buf = jnp.zeros((n_blocks * MOE_ROWS, d), x.dtype).at[dest].set(xt[tok])
    block_e = jnp.minimum(jnp.searchsorted(pad_end, jnp.arange(n_blocks) * MOE_ROWS, side='right'), MOE_EXPERTS - 1)

    def expert_rows(args):
        xb, e = args
        hid = jax.nn.silu(xb @ w_gate[e]) * (xb @ w_up[e])
        return hid @ w_down[e]

    yb = lax.map(expert_rows, (buf.reshape(n_blocks, MOE_ROWS, d), block_e))
    y_as = yb.reshape(n_blocks * MOE_ROWS, d)[dest] * gates.reshape(-1)[order][:, None].astype(x.dtype)
    y = jnp.zeros_like(xt).at[tok].add(y_as)
    return y.reshape(bsz, l, d)


def run_trunk(x, st_ssd, st_ssd_conv, st_gdn, st_gdn_conv, st_hgrn, w, block_ab, block_c):
    sm = jax.nn.softmax(w['hgrn_lb_logits'].astype(jnp.float32), axis=0)
    lower_bounds = jnp.cumsum(sm, axis=0) - sm[0]
    h = x
    n_ssd, n_ssd_conv, n_gdn, n_gdn_conv, n_hgrn = [], [], [], [], []
    for li in range(DEPTH):
        j = li // 2
        u = rms_norm(h, w['norm_mix'][li])
        if li % 2 == 0:
            mix, s1, c1, s2, c2 = ab_mixer(u, st_ssd[:, j], st_ssd_conv[:, j], st_gdn[:, j], st_gdn_conv[:, j], w, j, block_ab)
            n_ssd.append(s1)
            n_ssd_conv.append(c1)
            n_gdn.append(s2)
            n_gdn_conv.append(c2)
        else:
            mix, s3 = hgrn_mixer(u, st_hgrn[:, j], lower_bounds[li], w, j, block_c)
            n_hgrn.append(s3)
        h = h + mix
        ffn = hier_moe(rms_norm(h, w['norm_ffn'][li]), w['moe_w_group'][li], w['moe_w_router'][li],
                       w['moe_w_gate'][li], w['moe_w_up'][li], w['moe_w_down'][li])
        h = h + ffn.astype(h.dtype)
    y = rms_norm(h, w['norm_final'])
    return (y, jnp.stack(n_ssd, 1), jnp.stack(n_ssd_conv, 1), jnp.stack(n_gdn, 1),
            jnp.stack(n_gdn_conv, 1), jnp.stack(n_hgrn, 1))


def setup_inputs(seed: int = 0) -> dict:
    key = jax.random.key(seed)
    ks = iter(jax.random.split(key, 40))

    def nrm(shape, scale):
        return jax.random.normal(next(ks), shape, jnp.float32) * scale

    def gain(shape):
        return 1.0 + nrm(shape, 0.02)

    def dt_bias(shape):
        dt = jnp.exp(jax.random.uniform(next(ks), shape, jnp.float32, math.log(1e-3), math.log(1e-1)))
        return dt + jnp.log(-jnp.expm1(-dt))

    def a_log(shape):
        return jnp.log(jax.random.uniform(next(ks), shape, jnp.float32, 1.0, 16.0))

    return {
        'x_prompt': nrm((BATCH, SEQ, D_MODEL), 1.0),
        'x_sample': nrm((DEC_BATCH, DEC_SEQ, D_MODEL), 1.0),
        'state_ssd': nrm((DEC_BATCH, N_AB_LAYERS, SSD_HEADS, SSD_HEAD_DIM, SSD_STATE), 0.3),
        'state_ssd_conv': nrm((DEC_BATCH, N_AB_LAYERS, CONV_W - 1, SSD_CONV_DIM), 1.0),
        'state_gdn': nrm((DEC_BATCH, N_AB_LAYERS, GDN_V_HEADS, GDN_K_DIM, GDN_V_DIM), 0.1),
        'state_gdn_conv': nrm((DEC_BATCH, N_AB_LAYERS, CONV_W - 1, GDN_CONV_DIM), 1.0),
        'state_hgrn': nrm((DEC_BATCH, N_C_LAYERS, HGRN_HEADS, HGRN_K_DIM, HGRN_V_DIM), 0.3),
        'norm_mix': gain((DEPTH, D_MODEL)),
        'norm_ffn': gain((DEPTH, D_MODEL)),
        'norm_final': gain((D_MODEL,)),
        'w_in_ab': nrm((N_AB_LAYERS, D_MODEL, AB_IN_DIM), D_MODEL ** -0.5),
        'ssd_conv_w': nrm((N_AB_LAYERS, CONV_W, SSD_CONV_DIM), CONV_W ** -0.5),
        'ssd_conv_b': nrm((N_AB_LAYERS, SSD_CONV_DIM), 0.02),
        'ssd_dt_bias': dt_bias((N_AB_LAYERS, SSD_HEADS)),
        'ssd_a_log': a_log((N_AB_LAYERS, SSD_HEADS)),
        'ssd_d': 1.0 + nrm((N_AB_LAYERS, SSD_HEADS), 0.1),
        'ssd_norm': gain((N_AB_LAYERS, SSD_INNER)),
        'gdn_conv_w': nrm((N_AB_LAYERS, CONV_W, GDN_CONV_DIM), CONV_W ** -0.5),
        'gdn_dt_bias': dt_bias((N_AB_LAYERS, GDN_V_HEADS)),
        'gdn_a_log': a_log((N_AB_LAYERS, GDN_V_HEADS)),
        'gdn_norm': gain((N_AB_LAYERS, GDN_V_DIM)),
        'w_out_ab': nrm((N_AB_LAYERS, AB_MIX_DIM, D_MODEL), AB_MIX_DIM ** -0.5),
        'w_in_c': nrm((N_C_LAYERS, D_MODEL, HGRN_IN_DIM), D_MODEL ** -0.5),
        'hgrn_lb_logits': nrm((DEPTH, HGRN_HEADS * HGRN_K_DIM), 0.5),
        'hgrn_norm': gain((N_C_LAYERS, HGRN_V_DIM)),
        'w_out_c': nrm((N_C_LAYERS, HGRN_HEADS * HGRN_V_DIM, D_MODEL), (HGRN_HEADS * HGRN_V_DIM) ** -0.5),
        'moe_w_group': nrm((DEPTH, D_MODEL, MOE_GROUPS), D_MODEL ** -0.5),
        'moe_w_router': nrm((DEPTH, D_MODEL, MOE_EXPERTS), D_MODEL ** -0.5),
        'moe_w_gate': nrm((DEPTH, MOE_EXPERTS, D_MODEL, MOE_D_EXPERT), D_MODEL ** -0.5),
        'moe_w_up': nrm((DEPTH, MOE_EXPERTS, D_MODEL, MOE_D_EXPERT), D_MODEL ** -0.5),
        'moe_w_down': nrm((DEPTH, MOE_EXPERTS, MOE_D_EXPERT, D_MODEL), MOE_D_EXPERT ** -0.5),
    }


def reference(x_prompt, x_sample, state_ssd, state_ssd_conv, state_gdn, state_gdn_conv, state_hgrn,
              norm_mix, norm_ffn, norm_final,
              w_in_ab, ssd_conv_w, ssd_conv_b, ssd_dt_bias, ssd_a_log, ssd_d, ssd_norm,
              gdn_conv_w, gdn_dt_bias, gdn_a_log, gdn_norm, w_out_ab,
              w_in_c, hgrn_lb_logits, hgrn_norm, w_out_c,
              moe_w_group, moe_w_router, moe_w_gate, moe_w_up, moe_w_down):
    w = dict(norm_mix=norm_mix, norm_ffn=norm_ffn, norm_final=norm_final,
             w_in_ab=w_in_ab, ssd_conv_w=ssd_conv_w, ssd_conv_b=ssd_conv_b, ssd_dt_bias=ssd_dt_bias,
             ssd_a_log=ssd_a_log, ssd_d=ssd_d, ssd_norm=ssd_norm,
             gdn_conv_w=gdn_conv_w, gdn_dt_bias=gdn_dt_bias, gdn_a_log=gdn_a_log, gdn_norm=gdn_norm,
             w_out_ab=w_out_ab, w_in_c=w_in_c, hgrn_lb_logits=hgrn_lb_logits, hgrn_norm=hgrn_norm,
             w_out_c=w_out_c, moe_w_group=moe_w_group, moe_w_router=moe_w_router,
             moe_w_gate=moe_w_gate, moe_w_up=moe_w_up, moe_w_down=moe_w_down)
    bp, dtp = x_prompt.shape[0], x_prompt.dtype
    z_ssd = jnp.zeros((bp, N_AB_LAYERS, SSD_HEADS, SSD_HEAD_DIM, SSD_STATE), dtp)
    z_ssd_conv = jnp.zeros((bp, N_AB_LAYERS, CONV_W - 1, SSD_CONV_DIM), dtp)
    z_gdn = jnp.zeros((bp, N_AB_LAYERS, GDN_V_HEADS, GDN_K_DIM, GDN_V_DIM), dtp)
    z_gdn_conv = jnp.zeros((bp, N_AB_LAYERS, CONV_W - 1, GDN_CONV_DIM), dtp)
    z_hgrn = jnp.zeros((bp, N_C_LAYERS, HGRN_HEADS, HGRN_K_DIM, HGRN_V_DIM), dtp)
    y_prompt, p_ssd, p_ssd_conv, p_gdn, p_gdn_conv, p_hgrn = run_trunk(
        x_prompt, z_ssd, z_ssd_conv, z_gdn, z_gdn_conv, z_hgrn, w, CHUNK, HGRN_BLOCK)
    ls = x_sample.shape[1]
    y_sample, s_ssd, s_ssd_conv, s_gdn, s_gdn_conv, s_hgrn = run_trunk(
        x_sample, state_ssd, state_ssd_conv, state_gdn, state_gdn_conv, state_hgrn, w, ls, ls)
    return (y_prompt, y_sample, p_ssd, p_ssd_conv, p_gdn, p_gdn_conv, p_hgrn,
            s_ssd, s_ssd_conv, s_gdn, s_gdn_conv, s_hgrn)
```

```python
import functools

import jax
import jax.numpy as jnp
import numpy as np
from jax import lax
from jax.experimental import pallas as pl
from jax.experimental.pallas import tpu as pltpu

F32 = jnp.float32
BF16 = jnp.bfloat16

D_MODEL = 1024
EPS = 1e-6
CONV_W = 4
CONV_TAIL = CONV_W - 1
SSD_HEADS = 16
SSD_HEAD_DIM = 64
SSD_INNER = SSD_HEADS * SSD_HEAD_DIM
SSD_GROUPS = 2
SSD_STATE = 128
SSD_CONV_DIM = SSD_INNER + 2 * SSD_GROUPS * SSD_STATE
GDN_K_HEADS = 4
GDN_V_HEADS = 8
GDN_K_DIM = 128
GDN_V_DIM = 128
GDN_QK = GDN_K_HEADS * GDN_K_DIM
GDN_VD = GDN_V_HEADS * GDN_V_DIM
GDN_CONV_DIM = 2 * GDN_QK + GDN_VD
HGRN_HEADS = 8
HGRN_K_DIM = 128
HGRN_V_DIM = 128
HGRN_SUB = 16
MOE_GROUPS = 4
MOE_EPG = 8
MOE_EXPERTS = MOE_GROUPS * MOE_EPG
MOE_D_EXPERT = 512
MOE_ROW_TILE = 256
SMALL_W = 128
EXP_CLAMP = 80.0

LANE = 128
SUBLANE = 8
VMEM_LIMIT = 56 * 1024 * 1024


def _cparams(*sem):
    return pltpu.CompilerParams(dimension_semantics=sem, vmem_limit_bytes=VMEM_LIMIT)


def _dot(a, b):
    return jnp.dot(a, b, preferred_element_type=F32)


def _dot_nt(a, b):
    return lax.dot_general(a, b, (((1,), (1,)), ((), ())), preferred_element_type=F32)


def _split(x):
    hi = x.astype(BF16)
    lo = (x - hi.astype(F32)).astype(BF16)
    return hi, lo


def _dot3(a, b):
    ah, al = _split(a)
    bh, bl = _split(b)
    return _dot(ah, bh) + (_dot(ah, bl) + _dot(al, bh))


def _dot3_nt(a, b):
    ah, al = _split(a)
    bh, bl = _split(b)
    return _dot_nt(ah, bh) + (_dot_nt(ah, bl) + _dot_nt(al, bh))


def _dot2_exact_lhs(m_bf16, x):
    xh, xl = _split(x)
    return _dot(m_bf16, xh) + _dot(m_bf16, xl)


def _silu(x):
    return x * jax.nn.sigmoid(x)


def _softplus(x):
    return jnp.maximum(x, 0.0) + jnp.log1p(jnp.exp(-jnp.abs(x)))


def _iota2(shape, dim):
    return lax.broadcasted_iota(jnp.int32, shape, dim)


def _norm_proj_kernel(*refs, n_out):
    x_ref, g_ref = refs[:2]
    w_refs = refs[2:2 + n_out]
    o_refs = refs[2 + n_out:]
    x = x_ref[...]
    ms = jnp.mean(x * x, axis=-1, keepdims=True)
    u = (x * lax.rsqrt(ms + EPS) * g_ref[...]).astype(BF16)
    for w_ref, o_ref in zip(w_refs, o_refs):
        o_ref[...] = _dot(u, w_ref[...]).astype(o_ref.dtype)


def norm_proj(x, gain, ws, dtypes, tm, name):
    t, d = x.shape
    assert t % tm == 0
    return pl.pallas_call(
        functools.partial(_norm_proj_kernel, n_out=len(ws)),
        grid=(t // tm,),
        in_specs=([pl.BlockSpec((tm, d), lambda i: (i, 0)), pl.BlockSpec((1, d), lambda i: (0, 0))]
                  + [pl.BlockSpec(w.shape, lambda i: (0, 0)) for w in ws]),
        out_specs=[pl.BlockSpec((tm, w.shape[1]), lambda i: (i, 0)) for w in ws],
        out_shape=[jax.ShapeDtypeStruct((t, w.shape[1]), dt) for w, dt in zip(ws, dtypes)],
        compiler_params=_cparams("parallel"),
        name=name,
    )(x, gain.reshape(1, d), *ws)


def _conv_chunk(x_ref, c0_ref, cn_ref, ext_ref, w_ref, first, last, q):
    @pl.when(first)
    def _():
        ext_ref[SUBLANE - CONV_TAIL:SUBLANE, :] = c0_ref[0]

    ext_ref[SUBLANE:SUBLANE + q, :] = x_ref[...].astype(F32)
    acc = ext_ref[SUBLANE:SUBLANE + q, :] * w_ref[CONV_TAIL:CONV_W, :]
    for j in range(CONV_TAIL):
        lo = SUBLANE - CONV_TAIL + j
        acc = acc + ext_ref[lo:lo + q, :] * w_ref[j:j + 1, :]
    tail = ext_ref[SUBLANE + q - CONV_TAIL:SUBLANE + q, :]

    @pl.when(last)
    def _():
        cn_ref[0] = tail

    ext_ref[SUBLANE - CONV_TAIL:SUBLANE, :] = tail
    return acc


def _ssd_kernel(xbc_ref, sm_ref, z_ref, s0_ref, c0_ref, cw_ref, cb_ref, dtb_ref, dtbt_ref,
                alog_ref, alogt_ref, dfull_ref, nw_ref, exp_ref,
                y_ref, sn_ref, cn_ref, ext_ref, st_ref, *, q, nc):
    c = pl.program_id(1)
    first = c == 0
    last = c == nc - 1
    hp = SSD_HEADS // SSD_GROUPS * SSD_HEAD_DIM
    npair = SSD_HEADS // 2

    @pl.when(first)
    def _():
        for pr in range(npair):
            g, j = divmod(pr, npair // SSD_GROUPS)
            st_ref[g, :, j * LANE:(j + 1) * LANE] = s0_ref[0, pr].T

    xbc = _conv_chunk(xbc_ref, c0_ref, cn_ref, ext_ref, cw_ref, first, last, q) + cb_ref[...]
    xbc = _silu(xbc)
    xs = xbc[:, :SSD_INNER]
    xs_b = xs.astype(BF16)

    sm = sm_ref[...]
    sm_t = sm.T
    dt = _softplus(sm[:, :SSD_HEADS] + dtb_ref[...])
    dt_t = _softplus(sm_t[:SSD_HEADS, :] + dtbt_ref[...])
    a = -jnp.exp(alog_ref[...]) * dt
    a_t = -jnp.exp(alogt_ref[...]) * dt_t
    row = _iota2((q, q), 0)
    col = _iota2((q, q), 1)
    causal = row >= col
    low = jnp.where(causal, 1.0, 0.0).astype(BF16)
    upp = jnp.where(row <= col, 1.0, 0.0).astype(BF16)
    cum = _dot2_exact_lhs(low, a)
    ah, al = _split(a_t)
    cum_t = _dot(ah, upp) + _dot(al, upp)
    cum_last = cum[q - 1:q, :]
    ecum = jnp.exp(cum)
    tail = jnp.exp(cum_last - cum) * dt
    ecum_f = _dot3(ecum, exp_ref[...])
    tail_f = _dot3(tail, exp_ref[...])
    dlast_f = _dot3(jnp.exp(cum_last), exp_ref[...])

    lane = _iota2((q, LANE), 1)
    y_parts = []
    for g in range(SSD_GROUPS):
        bm = xbc[:, SSD_INNER + g * SSD_STATE:SSD_INNER + (g + 1) * SSD_STATE]
        cm = xbc[:, SSD_INNER + (SSD_GROUPS + g) * SSD_STATE:SSD_INNER + (SSD_GROUPS + g + 1) * SSD_STATE]
        bm_b = bm.astype(BF16)
        cm_b = cm.astype(BF16)
        gmat = _dot_nt(cm_b, bm_b)
        st = st_ref[g]
        y_int = _dot(cm_b, st.astype(BF16)) * ecum_f[:, g * hp:(g + 1) * hp]
        for j in range(npair // SSD_GROUPS):
            pr = g * (npair // SSD_GROUPS) + j
            xp = xs_b[:, pr * LANE:(pr + 1) * LANE]
            ys = []
            for hh in range(2):
                h = 2 * pr + hh
                diff = cum[:, h:h + 1] - cum_t[h:h + 1, :]
                dec = jnp.exp(jnp.where(causal, diff, -jnp.inf)) * dt_t[h:h + 1, :]
                ys.append(_dot((gmat * dec).astype(BF16), xp))
            y_parts.append(jnp.where(lane < SSD_HEAD_DIM, ys[0], ys[1]) + y_int[:, j * LANE:(j + 1) * LANE])
        xsc = (xs[:, g * hp:(g + 1) * hp] * tail_f[:, g * hp:(g + 1) * hp]).astype(BF16)
        st_ref[g] = st * dlast_f[:, g * hp:(g + 1) * hp] + _dot(bm.T.astype(BF16), xsc)

    y = jnp.concatenate(y_parts, axis=1) + dfull_ref[...] * xs
    y = y * _silu(z_ref[...].astype(F32))
    outs = []
    for g in range(SSD_GROUPS):
        yg = y[:, g * hp:(g + 1) * hp]
        ms = jnp.mean(yg * yg, axis=-1, keepdims=True)
        outs.append(yg * lax.rsqrt(ms + EPS))
    y_ref[...] = (jnp.concatenate(outs, axis=1) * nw_ref[...]).astype(y_ref.dtype)

    @pl.when(last)
    def _():
        for pr in range(npair):
            g, j = divmod(pr, npair // SSD_GROUPS)
            sn_ref[0, pr] = st_ref[g, :, j * LANE:(j + 1) * LANE].T


def ssd_scan(xbc, small, z, s0, c0, p, row0, bsz, seqlen, q, y_prev=None):
    nc = seqlen // q
    blk0 = row0 // q
    assert row0 % q == 0 and seqlen % q == 0
    npair = SSD_HEADS // 2
    tok = lambda b, c: (blk0 + b * nc + c, 0)
    fixed2 = lambda b, c: (0, 0)
    t = xbc.shape[0]
    in_specs = [
        pl.BlockSpec((q, SSD_CONV_DIM), tok),
        pl.BlockSpec((q, SMALL_W), tok),
        pl.BlockSpec((q, SSD_INNER), tok),
        pl.BlockSpec((1, npair, LANE, SSD_STATE), lambda b, c: (b, 0, 0, 0)),
        pl.BlockSpec((1, CONV_TAIL, SSD_CONV_DIM), lambda b, c: (b, 0, 0)),
        pl.BlockSpec((CONV_W, SSD_CONV_DIM), fixed2),
        pl.BlockSpec((1, SSD_CONV_DIM), fixed2),
        pl.BlockSpec((1, SSD_HEADS), fixed2),
        pl.BlockSpec((SSD_HEADS, 1), fixed2),
        pl.BlockSpec((1, SSD_HEADS), fixed2),
        pl.BlockSpec((SSD_HEADS, 1), fixed2),
        pl.BlockSpec((1, SSD_INNER), fixed2),
        pl.BlockSpec((1, SSD_INNER), fixed2),
        pl.BlockSpec((SSD_HEADS, SSD_INNER), fixed2),
    ]
    args = [xbc, small, z, s0.reshape(bsz, npair, LANE, SSD_STATE), c0,
            p['ssd_conv_w'], p['ssd_conv_b'], p['ssd_dt_bias'], p['ssd_dt_bias'].reshape(-1, 1),
            p['ssd_a_log'], p['ssd_a_log'].reshape(-1, 1), p['ssd_d_full'], p['ssd_norm'], p['ssd_expand']]
    aliases = {}
    if y_prev is not None:
        in_specs.append(pl.BlockSpec(memory_space=pl.ANY))
        args.append(y_prev)
        aliases = {len(args) - 1: 0}

    def body(*refs):
        if y_prev is not None:
            refs = refs[:14] + refs[15:]
        _ssd_kernel(*refs, q=q, nc=nc)

    y, sn, cn = pl.pallas_call(
        body,
        grid=(bsz, nc),
        in_specs=in_specs,
        out_specs=[pl.BlockSpec((q, SSD_INNER), tok),
                   pl.BlockSpec((1, npair, LANE, SSD_STATE), lambda b, c: (b, 0, 0, 0)),
                   pl.BlockSpec((1, CONV_TAIL, SSD_CONV_DIM), lambda b, c: (b, 0, 0))],
        out_shape=[jax.ShapeDtypeStruct((t, SSD_INNER), BF16),
                   jax.ShapeDtypeStruct((bsz, npair, LANE, SSD_STATE), F32),
                   jax.ShapeDtypeStruct((bsz, CONV_TAIL, SSD_CONV_DIM), F32)],
        scratch_shapes=[pltpu.VMEM((q + SUBLANE, SSD_CONV_DIM), F32),
                        pltpu.VMEM((SSD_GROUPS, SSD_STATE, SSD_INNER // SSD_GROUPS), F32)],
        input_output_aliases=aliases,
        compiler_params=_cparams("parallel", "arbitrary"),
        name="ssd_scan",
    )(*args)
    return y, sn.reshape(bsz, SSD_HEADS, SSD_HEAD_DIM, SSD_STATE), cn


def _solve_unit_lower(amats, rhss, q):
    ps = [-a for a in amats]
    ys = list(rhss)
    span = 1
    while span < q:
        last = 2 * span >= q
        nxt_p, nxt_y = [], []
        for p, y in zip(ps, ys):
            ph, plo = _split(p)
            yh, ylo = _split(y)
            if last:
                rh, rl = yh, ylo
            else:
                rh = jnp.concatenate([yh, ph], axis=1)
                rl = jnp.concatenate([ylo, plo], axis=1)
            z = _dot(ph, rh) + (_dot(ph, rl) + _dot(plo, rh))
            w = y.shape[1]
            nxt_y.append(y + z[:, :w])
            nxt_p.append(None if last else z[:, w:])
        ps, ys = nxt_p, nxt_y
        span *= 2
    return ys


def _gdn_kernel(qkv_ref, sm_ref, z_ref, s0_ref, c0_ref, cw_ref, dtb_ref, dtbt_ref,
                alog_ref, alogt_ref, nw_ref,
                o_ref, sn_ref, cn_ref, ext_ref, st_ref, *, q, nc):
    c = pl.program_id(1)
    first = c == 0
    last = c == nc - 1
    rep = GDN_V_HEADS // GDN_K_HEADS
    b_off = SSD_HEADS
    a_off = SSD_HEADS + GDN_V_HEADS

    @pl.when(first)
    def _():
        st_ref[...] = s0_ref[0]

    qkv = _silu(_conv_chunk(qkv_ref, c0_ref, cn_ref, ext_ref, cw_ref, first, last, q))
    sm = sm_ref[...]
    sm_t = sm.T
    beta = jax.nn.sigmoid(sm[:, b_off:b_off + GDN_V_HEADS])
    g = -jnp.exp(alog_ref[...]) * _softplus(sm[:, a_off:a_off + GDN_V_HEADS] + dtb_ref[...])
    g_t = -jnp.exp(alogt_ref[...]) * _softplus(sm_t[a_off:a_off + GDN_V_HEADS, :] + dtbt_ref[...])
    row = _iota2((q, q), 0)
    col = _iota2((q, q), 1)
    causal = row >= col
    strict = row > col
    low = jnp.where(causal, 1.0, 0.0).astype(BF16)
    upp = jnp.where(row <= col, 1.0, 0.0).astype(BF16)
    gam = _dot2_exact_lhs(low, g)
    gh, gl = _split(g_t)
    gam_t = _dot(gh, upp) + _dot(gl, upp)
    gam_last = gam[q - 1:q, :]
    eg = jnp.exp(gam)
    tail = jnp.exp(gam_last - gam)
    elast = jnp.exp(gam_last)

    qk_l, kk_l, kn_l, qk_rows = [], [], [], []
    for kh in range(GDN_K_HEADS):
        qh = qkv[:, kh * GDN_K_DIM:(kh + 1) * GDN_K_DIM]
        kx = qkv[:, GDN_QK + kh * GDN_K_DIM:GDN_QK + (kh + 1) * GDN_K_DIM]
        qn = qh * lax.rsqrt(jnp.sum(qh * qh, axis=-1, keepdims=True) + EPS) * (GDN_K_DIM ** -0.5)
        kn = kx * lax.rsqrt(jnp.sum(kx * kx, axis=-1, keepdims=True) + EPS)
        both = jnp.concatenate([qn.astype(BF16), kn.astype(BF16)], axis=0)
        prod = _dot_nt(both, both[q:])
        qk_l.append(prod[:q])
        kk_l.append(prod[q:])
        kn_l.append(kn)
        qk_rows.append(both)
    s_l = [st_ref[h] for h in range(GDN_V_HEADS)]
    qs_ks = [_dot(qk_rows[h // rep], s_l[h].astype(BF16)) for h in range(GDN_V_HEADS)]
    amats, rhss, decs = [], [], []
    for h in range(GDN_V_HEADS):
        v = qkv[:, 2 * GDN_QK + h * GDN_V_DIM:2 * GDN_QK + (h + 1) * GDN_V_DIM]
        diff = gam[:, h:h + 1] - gam_t[h:h + 1, :]
        dec = jnp.exp(jnp.where(causal, diff, -jnp.inf))
        bh = beta[:, h:h + 1]
        amats.append(jnp.where(strict, bh * dec * kk_l[h // rep], 0.0))
        rhss.append(bh * (v - eg[:, h:h + 1] * qs_ks[h][q:]))
        decs.append(dec)
    us = _solve_unit_lower(amats, rhss, q)
    z = z_ref[...].astype(F32)
    outs = []
    for h in range(GDN_V_HEADS):
        u_b = us[h].astype(BF16)
        o = _dot((qk_l[h // rep] * decs[h]).astype(BF16), u_b) + eg[:, h:h + 1] * qs_ks[h][:q]
        kt = (kn_l[h // rep] * tail[:, h:h + 1]).T.astype(BF16)
        st_ref[h] = s_l[h] * elast[:, h:h + 1] + _dot(kt, u_b)
        ms = jnp.mean(o * o, axis=-1, keepdims=True)
        zh = z[:, h * GDN_V_DIM:(h + 1) * GDN_V_DIM]
        outs.append(o * lax.rsqrt(ms + EPS) * nw_ref[...] * _silu(zh))
    o_ref[...] = jnp.concatenate(outs, axis=1).astype(o_ref.dtype)

    @pl.when(last)
    def _():
        sn_ref[0] = st_ref[...]


def gdn_scan(qkv, small, z, s0, c0, p, row0, bsz, seqlen, q, o_prev=None):
    nc = seqlen // q
    blk0 = row0 // q
    assert row0 % q == 0 and seqlen % q == 0
    tok = lambda b, c: (blk0 + b * nc + c, 0)
    fixed2 = lambda b, c: (0, 0)
    t = qkv.shape[0]
    st_spec = pl.BlockSpec((1, GDN_V_HEADS, GDN_K_DIM, GDN_V_DIM), lambda b, c: (b, 0, 0, 0))
    cv_spec = pl.BlockSpec((1, CONV_TAIL, GDN_CONV_DIM), lambda b, c: (b, 0, 0))
    in_specs = [
        pl.BlockSpec((q, GDN_CONV_DIM), tok),
        pl.BlockSpec((q, SMALL_W), tok),
        pl.BlockSpec((q, GDN_VD), tok),
        st_spec, cv_spec,
        pl.BlockSpec((CONV_W, GDN_CONV_DIM), fixed2),
        pl.BlockSpec((1, GDN_V_HEADS), fixed2),
        pl.BlockSpec((GDN_V_HEADS, 1), fixed2),
        pl.BlockSpec((1, GDN_V_HEADS), fixed2),
        pl.BlockSpec((GDN_V_HEADS, 1), fixed2),
        pl.BlockSpec((1, GDN_V_DIM), fixed2),
    ]
    args = [qkv, small, z, s0, c0, p['gdn_conv_w'], p['gdn_dt_bias'], p['gdn_dt_bias'].reshape(-1, 1),
            p['gdn_a_log'], p['gdn_a_log'].reshape(-1, 1), p['gdn_norm']]
    aliases = {}
    if o_prev is not None:
        in_specs.append(pl.BlockSpec(memory_space=pl.ANY))
        args.append(o_prev)
        aliases = {len(args) - 1: 0}

    def body(*refs):
        if o_prev is not None:
            refs = refs[:11] + refs[12:]
        _gdn_kernel(*refs, q=q, nc=nc)

    return pl.pallas_call(
        body,
        grid=(bsz, nc),
        in_specs=in_specs,
        out_specs=[pl.BlockSpec((q, GDN_VD), tok), st_spec, cv_spec],
        out_shape=[jax.ShapeDtypeStruct((t, GDN_VD), BF16),
                   jax.ShapeDtypeStruct((bsz, GDN_V_HEADS, GDN_K_DIM, GDN_V_DIM), F32),
                   jax.ShapeDtypeStruct((bsz, CONV_TAIL, GDN_CONV_DIM), F32)],
        scratch_shapes=[pltpu.VMEM((q + SUBLANE, GDN_CONV_DIM), F32),
                        pltpu.VMEM((GDN_V_HEADS, GDN_K_DIM, GDN_V_DIM), F32)],
        input_output_aliases=aliases,
        compiler_params=_cparams("parallel", "arbitrary"),
        name="gdn_scan",
    )(*args)


def _hgrn_kernel(q_ref, f_ref, i_ref, gz_ref, s0_ref, lb_ref, nw_ref,
                 o_ref, sn_ref, st_ref, *, q, nc):
    c = pl.program_id(1)
    nsub = q // HGRN_SUB

    @pl.when(c == 0)
    def _():
        for h in range(HGRN_HEADS):
            st_ref[h] = s0_ref[0, h].T

    lb = lb_ref[...]
    f = lb + (1.0 - lb) * jax.nn.sigmoid(f_ref[...])
    logf = jnp.log(f)
    kk = 1.0 - f
    qs = _silu(q_ref[...].astype(F32))
    row = _iota2((q, q), 0)
    col = _iota2((q, q), 1)
    same = (row // HGRN_SUB) == (col // HGRN_SUB)
    intra = same & (row >= col)
    m_pref = jnp.where(intra, 1.0, 0.0).astype(BF16)
    m_blk = jnp.where(same, 1.0, 0.0).astype(BF16)
    lh, ll = _split(logf)
    gam = _dot(m_pref, lh) + _dot(m_pref, ll)
    gtot = _dot(m_blk, lh) + _dot(m_blk, ll)
    qt = qs * jnp.exp(gam)
    kt = kk * jnp.exp(jnp.minimum(-gam, EXP_CLAMP))
    kd = kk * jnp.exp(gtot - gam)
    edec = jnp.exp(gtot)
    vv = i_ref[...].astype(F32)
    gz = gz_ref[...].astype(F32)
    rsub = _iota2((q, 1), 0) // HGRN_SUB

    heads = range(HGRN_HEADS)
    sls = [slice(h * HGRN_K_DIM, (h + 1) * HGRN_K_DIM) for h in heads]
    qt_b = [qt[:, sl].astype(BF16) for sl in sls]
    v_t = [vv[:, sl].T.astype(BF16) for sl in sls]
    o_in = []
    for h in heads:
        att = jnp.where(intra, _dot_nt(qt_b[h], kt[:, sls[h]].astype(BF16)), 0.0)
        o_in.append(_dot(att.astype(BF16), vv[:, sls[h]].astype(BF16)))

    def window_update(j):
        return [_dot(v_t[h], jnp.where(rsub == j, kd[:, sls[h]], 0.0).astype(BF16)) for h in heads]

    s_t = [st_ref[h] for h in heads]
    o_x = [[] for _ in heads]
    upd_next = window_update(0)
    for j in range(nsub):
        upd = upd_next
        if j + 1 < nsub:
            upd_next = window_update(j + 1)
        r0 = j * HGRN_SUB
        for h in heads:
            o_x[h].append(_dot_nt(qt_b[h][r0:r0 + HGRN_SUB, :], s_t[h].astype(BF16)))
            s_t[h] = s_t[h] * edec[r0:r0 + 1, sls[h]] + upd[h]
    outs = []
    for h in heads:
        st_ref[h] = s_t[h]
        o = o_in[h] + jnp.concatenate(o_x[h], axis=0)
        ms = jnp.mean(o * o, axis=-1, keepdims=True)
        outs.append(o * lax.rsqrt(ms + EPS) * nw_ref[...] * _silu(gz[:, sls[h]]))
    o_ref[...] = jnp.concatenate(outs, axis=1).astype(o_ref.dtype)

    @pl.when(c == nc - 1)
    def _():
        for h in range(HGRN_HEADS):
            sn_ref[0, h] = st_ref[h].T


def hgrn_scan(qp, fz, iv, gz, s0, lb, nw, row0, bsz, seqlen, q, o_prev=None):
    nc = seqlen // q
    blk0 = row0 // q
    assert row0 % q == 0 and seqlen % q == 0 and q % HGRN_SUB == 0
    hk = HGRN_HEADS * HGRN_K_DIM
    hv = HGRN_HEADS * HGRN_V_DIM
    tok = lambda b, c: (blk0 + b * nc + c, 0)
    fixed2 = lambda b, c: (0, 0)
    t = qp.shape[0]
    st_spec = pl.BlockSpec((1, HGRN_HEADS, HGRN_K_DIM, HGRN_V_DIM), lambda b, c: (b, 0, 0, 0))
    in_specs = [pl.BlockSpec((q, hk), tok), pl.BlockSpec((q, hk), tok),
                pl.BlockSpec((q, hv), tok), pl.BlockSpec((q, hv), tok),
                st_spec, pl.BlockSpec((1, hk), fixed2), pl.BlockSpec((1, HGRN_V_DIM), fixed2)]
    args = [qp, fz, iv, gz, s0, lb, nw]
    aliases = {}
    if o_prev is not None:
        in_specs.append(pl.BlockSpec(memory_space=pl.ANY))
        args.append(o_prev)
        aliases = {len(args) - 1: 0}

    def body(*refs):
        if o_prev is not None:
            refs = refs[:7] + refs[8:]
        _hgrn_kernel(*refs, q=q, nc=nc)

    return pl.pallas_call(
        body,
        grid=(bsz, nc),
        in_specs=in_specs,
        out_specs=[pl.BlockSpec((q, hv), tok), st_spec],
        out_shape=[jax.ShapeDtypeStruct((t, hv), BF16),
                   jax.ShapeDtypeStruct((bsz, HGRN_HEADS, HGRN_K_DIM, HGRN_V_DIM), F32)],
        scratch_shapes=[pltpu.VMEM((HGRN_HEADS, HGRN_V_DIM, HGRN_K_DIM), F32)],
        input_output_aliases=aliases,
        compiler_params=_cparams("parallel", "arbitrary"),
        name="hgrn_scan",
    )(*args)


def _out_proj_kernel(*refs, n_in):
    a_refs = refs[:n_in]
    w_refs = refs[n_in:2 * n_in]
    h_ref, o_ref = refs[2 * n_in], refs[2 * n_in + 1]
    acc = h_ref[...]
    for a_ref, w_ref in zip(a_refs, w_refs):
        acc = acc + _dot(a_ref[...], w_ref[...])
    o_ref[...] = acc


def out_proj(acts, ws, h, tm):
    t, d = h.shape
    n_in = len(acts)
    return pl.pallas_call(
        functools.partial(_out_proj_kernel, n_in=n_in),
        grid=(t // tm,),
        in_specs=([pl.BlockSpec((tm, a.shape[1]), lambda i: (i, 0)) for a in acts]
                  + [pl.BlockSpec(w.shape, lambda i: (0, 0)) for w in ws]
                  + [pl.BlockSpec((tm, d), lambda i: (i, 0))]),
        out_specs=pl.BlockSpec((tm, d), lambda i: (i, 0)),
        out_shape=jax.ShapeDtypeStruct((t, d), F32),
        compiler_params=_cparams("parallel"),
        name="out_proj",
    )(*acts, *ws, h)


R_E0, R_E1, R_RANK0, R_RANK1, R_GATE0, R_GATE1 = range(6)
ROUTE_ROWS = 8
ROUTER_ROWS = SUBLANE + MOE_EXPERTS


def _router_kernel(h_ref, g_ref, w_ref, xn_ref, info_ref, cnt_ref, carry_ref, *, tm, nt):
    i = pl.program_id(0)

    @pl.when(i == 0)
    def _():
        carry_ref[...] = jnp.zeros_like(carry_ref)

    x = h_ref[...]
    ms = jnp.mean(x * x, axis=-1, keepdims=True)
    xn = x * lax.rsqrt(ms + EPS) * g_ref[...]
    xn_ref[...] = xn
    logit = _dot3_nt(w_ref[...], xn)
    lg = logit[:MOE_GROUPS, :]
    gmax = jnp.max(lg, axis=0, keepdims=True)
    gidx = _iota2(lg.shape, 0)
    g_sel = jnp.min(jnp.where(lg == gmax, gidx, MOE_GROUPS), axis=0, keepdims=True)
    g_prob = 1.0 / jnp.sum(jnp.exp(lg - gmax), axis=0, keepdims=True)
    el = jnp.zeros((MOE_EPG, tm), F32)
    for g in range(MOE_GROUPS):
        el = jnp.where(g_sel == g, logit[SUBLANE + g * MOE_EPG:SUBLANE + (g + 1) * MOE_EPG, :], el)
    eidx = _iota2(el.shape, 0)
    m1 = jnp.max(el, axis=0, keepdims=True)
    i1 = jnp.min(jnp.where(el == m1, eidx, MOE_EPG), axis=0, keepdims=True)
    el2 = jnp.where(eidx == i1, -jnp.inf, el)
    m2 = jnp.max(el2, axis=0, keepdims=True)
    i2 = jnp.min(jnp.where(el2 == m2, eidx, MOE_EPG), axis=0, keepdims=True)
    ex = jnp.exp(m2 - m1)
    gate0 = g_prob / (1.0 + ex)
    gate1 = g_prob * ex / (1.0 + ex)
    e0 = g_sel * MOE_EPG + i1
    e1 = g_sel * MOE_EPG + i2

    xid = _iota2((MOE_EXPERTS, tm), 0)
    oh0 = xid == e0
    oh1 = xid == e1
    onehot = jnp.where(oh0 | oh1, 1.0, 0.0).astype(BF16)
    upp = jnp.where(_iota2((tm, tm), 0) <= _iota2((tm, tm), 1), 1.0, 0.0).astype(BF16)
    pref = _dot(onehot, upp)
    base = carry_ref[:, 0:1] + pref - 1.0
    rank0 = jnp.sum(jnp.where(oh0, base, 0.0), axis=0, keepdims=True)
    rank1 = jnp.sum(jnp.where(oh1, base, 0.0), axis=0, keepdims=True)
    new_carry = carry_ref[...] + pref[:, tm - 1:tm]
    carry_ref[...] = new_carry

    info_ref[R_E0:R_E0 + 1, :] = e0.astype(F32)
    info_ref[R_E1:R_E1 + 1, :] = e1.astype(F32)
    info_ref[R_RANK0:R_RANK0 + 1, :] = rank0
    info_ref[R_RANK1:R_RANK1 + 1, :] = rank1
    info_ref[R_GATE0:R_GATE0 + 1, :] = gate0
    info_ref[R_GATE1:R_GATE1 + 1, :] = gate1
    info_ref[R_GATE1 + 1:ROUTE_ROWS, :] = jnp.zeros((ROUTE_ROWS - R_GATE1 - 1, tm), F32)

    @pl.when(i == nt - 1)
    def _():
        cnt_ref[...] = new_carry


def moe_router(h, gain, w_rt, tm):
    t, d = h.shape
    nt = t // tm
    return pl.pallas_call(
        functools.partial(_router_kernel, tm=tm, nt=nt),
        grid=(nt,),
        in_specs=[pl.BlockSpec((tm, d), lambda i: (i, 0)),
                  pl.BlockSpec((1, d), lambda i: (0, 0)),
                  pl.BlockSpec((ROUTER_ROWS, d), lambda i: (0, 0))],
        out_specs=[pl.BlockSpec((tm, d), lambda i: (i, 0)),
                   pl.BlockSpec((ROUTE_ROWS, tm), lambda i: (0, i)),
                   pl.BlockSpec((MOE_EXPERTS, LANE), lambda i: (0, 0))],
        out_shape=[jax.ShapeDtypeStruct((t, d), F32),
                   jax.ShapeDtypeStruct((ROUTE_ROWS, t), F32),
                   jax.ShapeDtypeStruct((MOE_EXPERTS, LANE), F32)],
        scratch_shapes=[pltpu.VMEM((MOE_EXPERTS, LANE), F32)],
        compiler_params=_cparams("arbitrary"),
        name="moe_router",
    )(h, gain.reshape(1, d), w_rt)


DMA_UNROLL = 8


def _row_dma_loops(row_copy, tm):
    def start(t, carry):
        row_copy(t, 0).start(priority=0)
        row_copy(t, 1).start(priority=1)
        return carry

    def wait(t, carry):
        row_copy(t, 0).wait()
        row_copy(t, 1).wait()
        return carry

    return (lambda: lax.fori_loop(0, tm, start, 0, unroll=DMA_UNROLL),
            lambda: lax.fori_loop(0, tm, wait, 0, unroll=DMA_UNROLL))


def _dispatch_kernel(dest_ref, xn_ref, buf_in_ref, buf_ref, sem, *, tm):
    del buf_in_ref

    def row_copy(t, k):
        return pltpu.make_async_copy(xn_ref.at[pl.ds(t, 1)], buf_ref.at[pl.ds(dest_ref[0, k, t], 1)], sem)

    start_all, wait_all = _row_dma_loops(row_copy, tm)
    start_all()
    wait_all()


def moe_dispatch(xn, dest3, n_rows, tm):
    t, d = xn.shape
    buf0 = jnp.zeros((n_rows, d), xn.dtype)
    return pl.pallas_call(
        functools.partial(_dispatch_kernel, tm=tm),
        grid=(t // tm,),
        in_specs=[pl.BlockSpec((1, 2, tm), lambda i: (i, 0, 0), memory_space=pltpu.SMEM),
                  pl.BlockSpec((tm, d), lambda i: (i, 0)),
                  pl.BlockSpec(memory_space=pl.ANY)],
        out_specs=pl.BlockSpec(memory_space=pl.ANY),
        out_shape=jax.ShapeDtypeStruct((n_rows, d), xn.dtype),
        scratch_shapes=[pltpu.SemaphoreType.DMA(())],
        input_output_aliases={2: 0},
        compiler_params=_cparams("arbitrary"),
        name="moe_dispatch",
    )(dest3, xn, buf0)


def _expert_kernel(te_ref, nu_ref, x_ref, wg_ref, wu_ref, wd_ref, y_ref):
    i = pl.program_id(0)

    @pl.when(i < nu_ref[0])
    def _():
        x = x_ref[...].astype(BF16)
        hid = _silu(_dot(x, wg_ref[0].astype(BF16))) * _dot(x, wu_ref[0].astype(BF16))
        y_ref[...] = _dot(hid.astype(BF16), wd_ref[0].astype(BF16))

    @pl.when(i >= nu_ref[0])
    def _():
        y_ref[...] = jnp.zeros_like(y_ref)


def moe_experts(buf, tile_e, n_used, wg, wu, wd):
    n_rows, d = buf.shape
    n_tiles = n_rows // MOE_ROW_TILE
    return pl.pallas_call(
        _expert_kernel,
        grid_spec=pltpu.PrefetchScalarGridSpec(
            num_scalar_prefetch=2,
            grid=(n_tiles,),
            in_specs=[pl.BlockSpec((MOE_ROW_TILE, d), lambda i, te, nu: (i, 0)),
                      pl.BlockSpec((1, d, MOE_D_EXPERT), lambda i, te, nu: (te[i], 0, 0)),
                      pl.BlockSpec((1, d, MOE_D_EXPERT), lambda i, te, nu: (te[i], 0, 0)),
                      pl.BlockSpec((1, MOE_D_EXPERT, d), lambda i, te, nu: (te[i], 0, 0))],
            out_specs=pl.BlockSpec((MOE_ROW_TILE, d), lambda i, te, nu: (i, 0)),
        ),
        out_shape=jax.ShapeDtypeStruct((n_rows, d), F32),
        compiler_params=_cparams("arbitrary"),
        name="moe_experts",
    )(tile_e, n_used, buf, wg, wu, wd)


def _combine_kernel(dest_ref, h_ref, info_ref, fg_ref, yb_ref, o_ref, ybuf, sem, *, tm, final_norm):
    def row_copy(t, k):
        return pltpu.make_async_copy(yb_ref.at[pl.ds(dest_ref[0, k, t], 1)], ybuf.at[k, pl.ds(t, 1)], sem)

    start_all, wait_all = _row_dma_loops(row_copy, tm)
    start_all()
    info_t = info_ref[...].T
    g0 = info_t[:, R_GATE0:R_GATE0 + 1]
    g1 = info_t[:, R_GATE1:R_GATE1 + 1]
    wait_all()
    y = h_ref[...] + (g0 * ybuf[0] + g1 * ybuf[1])
    if final_norm:
        ms = jnp.mean(y * y, axis=-1, keepdims=True)
        y = y * lax.rsqrt(ms + EPS) * fg_ref[...]
    o_ref[...] = y


def moe_combine(h, info, dest3, yb, final_gain, final_norm, tm):
    t, d = h.shape
    return pl.pallas_call(
        functools.partial(_combine_kernel, tm=tm, final_norm=final_norm),
        grid=(t // tm,),
        in_specs=[pl.BlockSpec((1, 2, tm), lambda i: (i, 0, 0), memory_space=pltpu.SMEM),
                  pl.BlockSpec((tm, d), lambda i: (i, 0)),
                  pl.BlockSpec((ROUTE_ROWS, tm), lambda i: (0, i)),
                  pl.BlockSpec((1, d), lambda i: (0, 0)),
                  pl.BlockSpec(memory_space=pl.ANY)],
        out_specs=pl.BlockSpec((tm, d), lambda i: (i, 0)),
        out_shape=jax.ShapeDtypeStruct((t, d), F32),
        scratch_shapes=[pltpu.VMEM((2, tm, d), F32), pltpu.SemaphoreType.DMA(())],
        compiler_params=_cparams("arbitrary"),
        name="moe_combine",
    )(dest3, h, info, final_gain.reshape(1, d), yb)


def hier_moe_residual(h, gain, w_rt, wg, wu, wd, final_gain, final_norm, tm):
    t, d = h.shape
    xn, info, cnt = moe_router(h, gain, w_rt, tm)
    counts = cnt[:, 0].astype(jnp.int32)
    padded = (counts + MOE_ROW_TILE - 1) // MOE_ROW_TILE * MOE_ROW_TILE
    pad_end = jnp.cumsum(padded)
    pad_start = pad_end - padded
    n_tiles = (2 * t) // MOE_ROW_TILE + MOE_EXPERTS
    n_used = (pad_end[-1] // MOE_ROW_TILE).astype(jnp.int32)
    tile_row = jnp.minimum(jnp.arange(n_tiles, dtype=jnp.int32), n_used - 1) * MOE_ROW_TILE
    tile_e = jnp.sum(tile_row[:, None] >= pad_end[None, :], axis=1).astype(jnp.int32)
    e01 = info[R_E0:R_E1 + 1].astype(jnp.int32)
    rank01 = info[R_RANK0:R_RANK1 + 1].astype(jnp.int32)
    onehot = e01[:, :, None] == jnp.arange(MOE_EXPERTS, dtype=jnp.int32)
    dest = rank01 + jnp.sum(jnp.where(onehot, pad_start, 0), axis=-1)
    dest3 = dest.reshape(2, t // tm, tm).transpose(1, 0, 2)
    buf = moe_dispatch(xn, dest3, n_tiles * MOE_ROW_TILE, tm)
    yb = moe_experts(buf, tile_e, n_used.reshape(1), wg, wu, wd)
    return moe_combine(h, info, dest3, yb, final_gain, final_norm, tm)


def _prep_ab_weights(w_in_ab):
    o = np.cumsum([0, SSD_INNER, SSD_CONV_DIM, SSD_HEADS, GDN_CONV_DIM, GDN_VD, GDN_V_HEADS, GDN_V_HEADS])
    wb = w_in_ab.astype(BF16)
    seg = lambda k: wb[:, o[k]:o[k + 1]]
    small = jnp.concatenate([seg(2), seg(5), seg(6)], axis=1)
    small = jnp.pad(small, ((0, 0), (0, SMALL_W - small.shape[1])))
    return [seg(0), seg(1), seg(3), seg(4), small]


def _trunk(x, states, w, bsz, seqlen, q_ssd, q_gdn, q_hgrn, tm):
    st_ssd, st_ssd_conv, st_gdn, st_gdn_conv, st_hgrn = states
    t = bsz * seqlen
    h = x.reshape(t, D_MODEL)

    zs, xbc, qkv, zg, small = norm_proj(h, w['norm_mix'][0], w['w_in_ab'], (BF16, BF16, BF16, BF16, F32), tm,
                                        "in_proj_ab")
    y, n_ssd, n_ssd_conv = ssd_scan(xbc, small, zs, st_ssd[:, 0], st_ssd_conv[:, 0], w, 0, bsz, seqlen, q_ssd)
    o, n_gdn, n_gdn_conv = gdn_scan(qkv, small, zg, st_gdn[:, 0], st_gdn_conv[:, 0], w, 0, bsz, seqlen, q_gdn)
    h = out_proj([y, o], [w['w_out_ab'][:SSD_INNER], w['w_out_ab'][SSD_INNER:]], h, tm)
    h = hier_moe_residual(h, w['norm_ffn'][0], w['moe_w_rt'][0], w['moe_w_gate'][0], w['moe_w_up'][0],
                          w['moe_w_down'][0], w['norm_final'], False, tm)

    qp, fz, iv, gz = norm_proj(h, w['norm_mix'][1], w['w_in_c'], (BF16, F32, BF16, BF16), tm, "in_proj_c")
    oc, n_hgrn = hgrn_scan(qp, fz, iv, gz, st_hgrn[:, 0], w['hgrn_lb'], w['hgrn_norm'], 0, bsz, seqlen, q_hgrn)
    h = out_proj([oc], [w['w_out_c']], h, tm)
    yout = hier_moe_residual(h, w['norm_ffn'][1], w['moe_w_rt'][1], w['moe_w_gate'][1], w['moe_w_up'][1],
                             w['moe_w_down'][1], w['norm_final'], True, tm).reshape(bsz, seqlen, D_MODEL)
    return (yout, n_ssd[:, None], n_ssd_conv[:, None], n_gdn[:, None], n_gdn_conv[:, None], n_hgrn[:, None])


def kernel(x_prompt, x_sample, state_ssd, state_ssd_conv, state_gdn, state_gdn_conv, state_hgrn,
           norm_mix, norm_ffn, norm_final,
           w_in_ab, ssd_conv_w, ssd_conv_b, ssd_dt_bias, ssd_a_log, ssd_d, ssd_norm,
           gdn_conv_w, gdn_dt_bias, gdn_a_log, gdn_norm, w_out_ab,
           w_in_c, hgrn_lb_logits, hgrn_norm, w_out_c,
           moe_w_group, moe_w_router, moe_w_gate, moe_w_up, moe_w_down):
    depth = norm_mix.shape[0]
    sm = jax.nn.softmax(hgrn_lb_logits.astype(F32), axis=0)
    lower_bounds = jnp.cumsum(sm, axis=0) - sm[0]
    w_rt = jnp.concatenate([jnp.swapaxes(moe_w_group, 1, 2),
                            jnp.zeros((depth, SUBLANE - MOE_GROUPS, D_MODEL), F32),
                            jnp.swapaxes(moe_w_router, 1, 2)], axis=1)
    expand = (jnp.arange(SSD_INNER)[None, :] // SSD_HEAD_DIM == jnp.arange(SSD_HEADS)[:, None]).astype(F32)
    w = dict(
        norm_mix=norm_mix, norm_ffn=norm_ffn, norm_final=norm_final,
        w_in_ab=_prep_ab_weights(w_in_ab[0]),
        ssd_conv_w=ssd_conv_w[0], ssd_conv_b=ssd_conv_b[0][None, :], ssd_dt_bias=ssd_dt_bias[0][None, :],
        ssd_a_log=ssd_a_log[0][None, :], ssd_d_full=jnp.repeat(ssd_d[0], SSD_HEAD_DIM)[None, :],
        ssd_norm=ssd_norm[0][None, :], ssd_expand=expand,
        gdn_conv_w=gdn_conv_w[0], gdn_dt_bias=gdn_dt_bias[0][None, :], gdn_a_log=gdn_a_log[0][None, :],
        gdn_norm=gdn_norm[0][None, :], w_out_ab=w_out_ab[0].astype(BF16),
        w_in_c=[w_in_c[0][:, k * D_MODEL:(k + 1) * D_MODEL].astype(BF16) for k in range(4)],
        hgrn_lb=lower_bounds[1][None, :], hgrn_norm=hgrn_norm[0][None, :],
        w_out_c=w_out_c[0].astype(BF16), moe_w_rt=w_rt,
        moe_w_gate=moe_w_gate, moe_w_up=moe_w_up, moe_w_down=moe_w_down,
    )
    bp, lp, _ = x_prompt.shape
    bs, ls, _ = x_sample.shape
    zeros = lambda ref: jnp.zeros((bp,) + ref.shape[1:], x_prompt.dtype)
    p_states = tuple(zeros(s) for s in (state_ssd, state_ssd_conv, state_gdn, state_gdn_conv, state_hgrn))
    s_states = (state_ssd, state_ssd_conv, state_gdn, state_gdn_conv, state_hgrn)
    outs_p = _trunk(x_prompt, p_states, w, bp, lp, 256, 64, 128, 512)
    outs_s = _trunk(x_sample, s_states, w, bs, ls, ls, ls, ls, bs * ls)
    return (outs_p[0], outs_s[0]) + tuple(outs_p[1:]) + tuple(outs_s[1:])
```

```python
import functools

import jax
import jax.numpy as jnp
import numpy as np
from jax import lax
from jax.experimental import pallas as pl
from jax.experimental.pallas import tpu as pltpu

F32 = jnp.float32
BF16 = jnp.bfloat16

D_MODEL = 1024
EPS = 1e-6
CONV_W = 4
CONV_TAIL = CONV_W - 1
SSD_HEADS = 16
SSD_HEAD_DIM = 64
SSD_INNER = SSD_HEADS * SSD_HEAD_DIM
SSD_GROUPS = 2
SSD_STATE = 128
SSD_CONV_DIM = SSD_INNER + 2 * SSD_GROUPS * SSD_STATE
GDN_K_HEADS = 4
GDN_V_HEADS = 8
GDN_K_DIM = 128
GDN_V_DIM = 128
GDN_QK = GDN_K_HEADS * GDN_K_DIM
GDN_VD = GDN_V_HEADS * GDN_V_DIM
GDN_CONV_DIM = 2 * GDN_QK + GDN_VD
HGRN_HEADS = 8
HGRN_K_DIM = 128
HGRN_V_DIM = 128
HGRN_SUB = 32
MOE_GROUPS = 4
MOE_EPG = 8
MOE_EXPERTS = MOE_GROUPS * MOE_EPG
MOE_D_EXPERT = 512
MOE_ROW_TILE = 256
SMALL_W = 128
EXP_CLAMP = 80.0

LANE = 128
SUBLANE = 8
VMEM_LIMIT = 56 * 1024 * 1024


def _cparams(*sem):
    return pltpu.CompilerParams(dimension_semantics=sem, vmem_limit_bytes=VMEM_LIMIT)


def _dot(a, b):
    return jnp.dot(a, b, preferred_element_type=F32)


def _dot_nt(a, b):
    return lax.dot_general(a, b, (((1,), (1,)), ((), ())), preferred_element_type=F32)


def _split(x):
    hi = x.astype(BF16)
    lo = (x - hi.astype(F32)).astype(BF16)
    return hi, lo


def _dot3(a, b):
    ah, al = _split(a)
    bh, bl = _split(b)
    return _dot(ah, bh) + (_dot(ah, bl) + _dot(al, bh))


def _dot3_nt(a, b):
    ah, al = _split(a)
    bh, bl = _split(b)
    return _dot_nt(ah, bh) + (_dot_nt(ah, bl) + _dot_nt(al, bh))


def _dot2_exact_lhs(m_bf16, x):
    xh, xl = _split(x)
    return _dot(m_bf16, xh) + _dot(m_bf16, xl)


def _silu(x):
    return x * jax.nn.sigmoid(x)


def _softplus(x):
    return jnp.maximum(x, 0.0) + jnp.log1p(jnp.exp(-jnp.abs(x)))


def _iota2(shape, dim):
    return lax.broadcasted_iota(jnp.int32, shape, dim)


def _norm_proj_kernel(*refs, n_out):
    x_ref, g_ref = refs[:2]
    w_refs = refs[2:2 + n_out]
    o_refs = refs[2 + n_out:]
    x = x_ref[...]
    ms = jnp.mean(x * x, axis=-1, keepdims=True)
    u = (x * lax.rsqrt(ms + EPS) * g_ref[...]).astype(BF16)
    for w_ref, o_ref in zip(w_refs, o_refs):
        o_ref[...] = _dot(u, w_ref[...]).astype(o_ref.dtype)


def norm_proj(x, gain, ws, dtypes, tm, name):
    t, d = x.shape
    assert t % tm == 0
    return pl.pallas_call(
        functools.partial(_norm_proj_kernel, n_out=len(ws)),
        grid=(t // tm,),
        in_specs=([pl.BlockSpec((tm, d), lambda i: (i, 0)), pl.BlockSpec((1, d), lambda i: (0, 0))]
                  + [pl.BlockSpec(w.shape, lambda i: (0, 0)) for w in ws]),
        out_specs=[pl.BlockSpec((tm, w.shape[1]), lambda i: (i, 0)) for w in ws],
        out_shape=[jax.ShapeDtypeStruct((t, w.shape[1]), dt) for w, dt in zip(ws, dtypes)],
        compiler_params=_cparams("parallel"),
        name=name,
    )(x, gain.reshape(1, d), *ws)


def _conv_chunk(x_ref, c0_ref, cn_ref, ext_ref, w_ref, first, last, q):
    @pl.when(first)
    def _():
        ext_ref[SUBLANE - CONV_TAIL:SUBLANE, :] = c0_ref[0]

    ext_ref[SUBLANE:SUBLANE + q, :] = x_ref[...].astype(F32)
    acc = ext_ref[SUBLANE:SUBLANE + q, :] * w_ref[CONV_TAIL:CONV_W, :]
    for j in range(CONV_TAIL):
        lo = SUBLANE - CONV_TAIL + j
        acc = acc + ext_ref[lo:lo + q, :] * w_ref[j:j + 1, :]
    tail = ext_ref[SUBLANE + q - CONV_TAIL:SUBLANE + q, :]

    @pl.when(last)
    def _():
        cn_ref[0] = tail

    ext_ref[SUBLANE - CONV_TAIL:SUBLANE, :] = tail
    return acc


def _ssd_kernel(xbc_ref, sm_ref, z_ref, s0_ref, c0_ref, cw_ref, cb_ref, dtb_ref, dtbt_ref,
                alog_ref, alogt_ref, dfull_ref, nw_ref, exp_ref,
                y_ref, sn_ref, cn_ref, ext_ref, st_ref, *, q, nc):
    c = pl.program_id(1)
    first = c == 0
    last = c == nc - 1
    hp = SSD_HEADS // SSD_GROUPS * SSD_HEAD_DIM
    npair = SSD_HEADS // 2

    @pl.when(first)
    def _():
        for pr in range(npair):
            g, j = divmod(pr, npair // SSD_GROUPS)
            st_ref[g, :, j * LANE:(j + 1) * LANE] = s0_ref[0, pr].T

    xbc = _conv_chunk(xbc_ref, c0_ref, cn_ref, ext_ref, cw_ref, first, last, q) + cb_ref[...]
    xbc = _silu(xbc)
    xs = xbc[:, :SSD_INNER]
    xs_b = xs.astype(BF16)

    sm = sm_ref[...]
    sm_t = sm.T
    dt = _softplus(sm[:, :SSD_HEADS] + dtb_ref[...])
    dt_t = _softplus(sm_t[:SSD_HEADS, :] + dtbt_ref[...])
    a = -jnp.exp(alog_ref[...]) * dt
    a_t = -jnp.exp(alogt_ref[...]) * dt_t
    row = _iota2((q, q), 0)
    col = _iota2((q, q), 1)
    causal = row >= col
    low = jnp.where(causal, 1.0, 0.0).astype(BF16)
    upp = jnp.where(row <= col, 1.0, 0.0).astype(BF16)
    cum = _dot2_exact_lhs(low, a)
    ah, al = _split(a_t)
    cum_t = _dot(ah, upp) + _dot(al, upp)
    cum_last = cum[q - 1:q, :]
    ecum = jnp.exp(cum)
    tail = jnp.exp(cum_last - cum) * dt
    ecum_f = _dot3(ecum, exp_ref[...])
    tail_f = _dot3(tail, exp_ref[...])
    dlast_f = _dot3(jnp.exp(cum_last), exp_ref[...])

    lane = _iota2((q, LANE), 1)
    y_parts = []
    for g in range(SSD_GROUPS):
        bm = xbc[:, SSD_INNER + g * SSD_STATE:SSD_INNER + (g + 1) * SSD_STATE]
        cm = xbc[:, SSD_INNER + (SSD_GROUPS + g) * SSD_STATE:SSD_INNER + (SSD_GROUPS + g + 1) * SSD_STATE]
        bm_b = bm.astype(BF16)
        cm_b = cm.astype(BF16)
        gmat = _dot_nt(cm_b, bm_b)
        st = st_ref[g]
        y_int = _dot(cm_b, st.astype(BF16)) * ecum_f[:, g * hp:(g + 1) * hp]
        for j in range(npair // SSD_GROUPS):
            pr = g * (npair // SSD_GROUPS) + j
            xp = xs_b[:, pr * LANE:(pr + 1) * LANE]
            ys = []
            for hh in range(2):
                h = 2 * pr + hh
                diff = cum[:, h:h + 1] - cum_t[h:h + 1, :]
                dec = jnp.exp(jnp.where(causal, diff, -jnp.inf)) * dt_t[h:h + 1, :]
                ys.append(_dot((gmat * dec).astype(BF16), xp))
            y_parts.append(jnp.where(lane < SSD_HEAD_DIM, ys[0], ys[1]) + y_int[:, j * LANE:(j + 1) * LANE])
        xsc = (xs[:, g * hp:(g + 1) * hp] * tail_f[:, g * hp:(g + 1) * hp]).astype(BF16)
        st_ref[g] = st * dlast_f[:, g * hp:(g + 1) * hp] + _dot(bm.T.astype(BF16), xsc)

    y = jnp.concatenate(y_parts, axis=1) + dfull_ref[...] * xs
    y = y * _silu(z_ref[...].astype(F32))
    outs = []
    for g in range(SSD_GROUPS):
        yg = y[:, g * hp:(g + 1) * hp]
        ms = jnp.mean(yg * yg, axis=-1, keepdims=True)
        outs.append(yg * lax.rsqrt(ms + EPS))
    y_ref[...] = (jnp.concatenate(outs, axis=1) * nw_ref[...]).astype(y_ref.dtype)

    @pl.when(last)
    def _():
        for pr in range(npair):
            g, j = divmod(pr, npair // SSD_GROUPS)
            sn_ref[0, pr] = st_ref[g, :, j * LANE:(j + 1) * LANE].T


def ssd_scan(xbc, small, z, s0, c0, p, row0, bsz, seqlen, q, y_prev=None):
    nc = seqlen // q
    blk0 = row0 // q
    assert row0 % q == 0 and seqlen % q == 0
    npair = SSD_HEADS // 2
    tok = lambda b, c: (blk0 + b * nc + c, 0)
    fixed2 = lambda b, c: (0, 0)
    t = xbc.shape[0]
    in_specs = [
        pl.BlockSpec((q, SSD_CONV_DIM), tok),
        pl.BlockSpec((q, SMALL_W), tok),
        pl.BlockSpec((q, SSD_INNER), tok),
        pl.BlockSpec((1, npair, LANE, SSD_STATE), lambda b, c: (b, 0, 0, 0)),
        pl.BlockSpec((1, CONV_TAIL, SSD_CONV_DIM), lambda b, c: (b, 0, 0)),
        pl.BlockSpec((CONV_W, SSD_CONV_DIM), fixed2),
        pl.BlockSpec((1, SSD_CONV_DIM), fixed2),
        pl.BlockSpec((1, SSD_HEADS), fixed2),
        pl.BlockSpec((SSD_HEADS, 1), fixed2),
        pl.BlockSpec((1, SSD_HEADS), fixed2),
        pl.BlockSpec((SSD_HEADS, 1), fixed2),
        pl.BlockSpec((1, SSD_INNER), fixed2),
        pl.BlockSpec((1, SSD_INNER), fixed2),
        pl.BlockSpec((SSD_HEADS, SSD_INNER), fixed2),
    ]
    args = [xbc, small, z, s0.reshape(bsz, npair, LANE, SSD_STATE), c0,
            p['ssd_conv_w'], p['ssd_conv_b'], p['ssd_dt_bias'], p['ssd_dt_bias'].reshape(-1, 1),
            p['ssd_a_log'], p['ssd_a_log'].reshape(-1, 1), p['ssd_d_full'], p['ssd_norm'], p['ssd_expand']]
    aliases = {}
    if y_prev is not None:
        in_specs.append(pl.BlockSpec(memory_space=pl.ANY))
        args.append(y_prev)
        aliases = {len(args) - 1: 0}

    def body(*refs):
        if y_prev is not None:
            refs = refs[:14] + refs[15:]
        _ssd_kernel(*refs, q=q, nc=nc)

    y, sn, cn = pl.pallas_call(
        body,
        grid=(bsz, nc),
        in_specs=in_specs,
        out_specs=[pl.BlockSpec((q, SSD_INNER), tok),
                   pl.BlockSpec((1, npair, LANE, SSD_STATE), lambda b, c: (b, 0, 0, 0)),
                   pl.BlockSpec((1, CONV_TAIL, SSD_CONV_DIM), lambda b, c: (b, 0, 0))],
        out_shape=[jax.ShapeDtypeStruct((t, SSD_INNER), BF16),
                   jax.ShapeDtypeStruct((bsz, npair, LANE, SSD_STATE), F32),
                   jax.ShapeDtypeStruct((bsz, CONV_TAIL, SSD_CONV_DIM), F32)],
        scratch_shapes=[pltpu.VMEM((q + SUBLANE, SSD_CONV_DIM), F32),
                        pltpu.VMEM((SSD_GROUPS, SSD_STATE, SSD_INNER // SSD_GROUPS), F32)],
        input_output_aliases=aliases,
        compiler_params=_cparams("parallel", "arbitrary"),
        name="ssd_scan",
    )(*args)
    return y, sn.reshape(bsz, SSD_HEADS, SSD_HEAD_DIM, SSD_STATE), cn


def _solve_unit_lower(amats, rhss, q):
    ps = [-a for a in amats]
    ys = list(rhss)
    span = 1
    while span < q:
        last = 2 * span >= q
        nxt_p, nxt_y = [], []
        for p, y in zip(ps, ys):
            ph, plo = _split(p)
            yh, ylo = _split(y)
            if last:
                rh, rl = yh, ylo
            else:
                rh = jnp.concatenate([yh, ph], axis=1)
                rl = jnp.concatenate([ylo, plo], axis=1)
            z = _dot(ph, rh) + (_dot(ph, rl) + _dot(plo, rh))
            w = y.shape[1]
            nxt_y.append(y + z[:, :w])
            nxt_p.append(None if last else z[:, w:])
        ps, ys = nxt_p, nxt_y
        span *= 2
    return ys


def _gdn_kernel(qkv_ref, sm_ref, z_ref, s0_ref, c0_ref, cw_ref, dtb_ref, dtbt_ref,
                alog_ref, alogt_ref, nw_ref,
                o_ref, sn_ref, cn_ref, ext_ref, st_ref, *, q, nc):
    c = pl.program_id(1)
    first = c == 0
    last = c == nc - 1
    rep = GDN_V_HEADS // GDN_K_HEADS
    b_off = SSD_HEADS
    a_off = SSD_HEADS + GDN_V_HEADS

    @pl.when(first)
    def _():
        st_ref[...] = s0_ref[0]

    qkv = _silu(_conv_chunk(qkv_ref, c0_ref, cn_ref, ext_ref, cw_ref, first, last, q))
    sm = sm_ref[...]
    sm_t = sm.T
    beta = jax.nn.sigmoid(sm[:, b_off:b_off + GDN_V_HEADS])
    g = -jnp.exp(alog_ref[...]) * _softplus(sm[:, a_off:a_off + GDN_V_HEADS] + dtb_ref[...])
    g_t = -jnp.exp(alogt_ref[...]) * _softplus(sm_t[a_off:a_off + GDN_V_HEADS, :] + dtbt_ref[...])
    row = _iota2((q, q), 0)
    col = _iota2((q, q), 1)
    causal = row >= col
    strict = row > col
    low = jnp.where(causal, 1.0, 0.0).astype(BF16)
    upp = jnp.where(row <= col, 1.0, 0.0).astype(BF16)
    gam = _dot2_exact_lhs(low, g)
    gh, gl = _split(g_t)
    gam_t = _dot(gh, upp) + _dot(gl, upp)
    gam_last = gam[q - 1:q, :]
    eg = jnp.exp(gam)
    tail = jnp.exp(gam_last - gam)
    elast = jnp.exp(gam_last)

    qk_l, kk_l, kn_l, qk_rows = [], [], [], []
    for kh in range(GDN_K_HEADS):
        qh = qkv[:, kh * GDN_K_DIM:(kh + 1) * GDN_K_DIM]
        kx = qkv[:, GDN_QK + kh * GDN_K_DIM:GDN_QK + (kh + 1) * GDN_K_DIM]
        qn = qh * lax.rsqrt(jnp.sum(qh * qh, axis=-1, keepdims=True) + EPS) * (GDN_K_DIM ** -0.5)
        kn = kx * lax.rsqrt(jnp.sum(kx * kx, axis=-1, keepdims=True) + EPS)
        both = jnp.concatenate([qn.astype(BF16), kn.astype(BF16)], axis=0)
        prod = _dot_nt(both, both[q:])
        qk_l.append(prod[:q])
        kk_l.append(prod[q:])
        kn_l.append(kn)
        qk_rows.append(both)
    s_l = [st_ref[h] for h in range(GDN_V_HEADS)]
    qs_ks = [_dot(qk_rows[h // rep], s_l[h].astype(BF16)) for h in range(GDN_V_HEADS)]
    amats, rhss, decs = [], [], []
    for h in range(GDN_V_HEADS):
        v = qkv[:, 2 * GDN_QK + h * GDN_V_DIM:2 * GDN_QK + (h + 1) * GDN_V_DIM]
        diff = gam[:, h:h + 1] - gam_t[h:h + 1, :]
        dec = jnp.exp(jnp.where(causal, diff, -jnp.inf))
        bh = beta[:, h:h + 1]
        amats.append(jnp.where(strict, bh * dec * kk_l[h // rep], 0.0))
        rhss.append(bh * (v - eg[:, h:h + 1] * qs_ks[h][q:]))
        decs.append(dec)
    us = _solve_unit_lower(amats, rhss, q)
    z = z_ref[...].astype(F32)
    outs = []
    for h in range(GDN_V_HEADS):
        u_b = us[h].astype(BF16)
        o = _dot((qk_l[h // rep] * decs[h]).astype(BF16), u_b) + eg[:, h:h + 1] * qs_ks[h][:q]
        kt = (kn_l[h // rep] * tail[:, h:h + 1]).T.astype(BF16)
        st_ref[h] = s_l[h] * elast[:, h:h + 1] + _dot(kt, u_b)
        ms = jnp.mean(o * o, axis=-1, keepdims=True)
        zh = z[:, h * GDN_V_DIM:(h + 1) * GDN_V_DIM]
        outs.append(o * lax.rsqrt(ms + EPS) * nw_ref[...] * _silu(zh))
    o_ref[...] = jnp.concatenate(outs, axis=1).astype(o_ref.dtype)

    @pl.when(last)
    def _():
        sn_ref[0] = st_ref[...]


def gdn_scan(qkv, small, z, s0, c0, p, row0, bsz, seqlen, q, o_prev=None):
    nc = seqlen // q
    blk0 = row0 // q
    assert row0 % q == 0 and seqlen % q == 0
    tok = lambda b, c: (blk0 + b * nc + c, 0)
    fixed2 = lambda b, c: (0, 0)
    t = qkv.shape[0]
    st_spec = pl.BlockSpec((1, GDN_V_HEADS, GDN_K_DIM, GDN_V_DIM), lambda b, c: (b, 0, 0, 0))
    cv_spec = pl.BlockSpec((1, CONV_TAIL, GDN_CONV_DIM), lambda b, c: (b, 0, 0))
    in_specs = [
        pl.BlockSpec((q, GDN_CONV_DIM), tok),
        pl.BlockSpec((q, SMALL_W), tok),
        pl.BlockSpec((q, GDN_VD), tok),
        st_spec, cv_spec,
        pl.BlockSpec((CONV_W, GDN_CONV_DIM), fixed2),
        pl.BlockSpec((1, GDN_V_HEADS), fixed2),
        pl.BlockSpec((GDN_V_HEADS, 1), fixed2),
        pl.BlockSpec((1, GDN_V_HEADS), fixed2),
        pl.BlockSpec((GDN_V_HEADS, 1), fixed2),
        pl.BlockSpec((1, GDN_V_DIM), fixed2),
    ]
    args = [qkv, small, z, s0, c0, p['gdn_conv_w'], p['gdn_dt_bias'], p['gdn_dt_bias'].reshape(-1, 1),
            p['gdn_a_log'], p['gdn_a_log'].reshape(-1, 1), p['gdn_norm']]
    aliases = {}
    if o_prev is not None:
        in_specs.append(pl.BlockSpec(memory_space=pl.ANY))
        args.append(o_prev)
        aliases = {len(args) - 1: 0}

    def body(*refs):
        if o_prev is not None:
            refs = refs[:11] + refs[12:]
        _gdn_kernel(*refs, q=q, nc=nc)

    return pl.pallas_call(
        body,
        grid=(bsz, nc),
        in_specs=in_specs,
        out_specs=[pl.BlockSpec((q, GDN_VD), tok), st_spec, cv_spec],
        out_shape=[jax.ShapeDtypeStruct((t, GDN_VD), BF16),
                   jax.ShapeDtypeStruct((bsz, GDN_V_HEADS, GDN_K_DIM, GDN_V_DIM), F32),
                   jax.ShapeDtypeStruct((bsz, CONV_TAIL, GDN_CONV_DIM), F32)],
        scratch_shapes=[pltpu.VMEM((q + SUBLANE, GDN_CONV_DIM), F32),
                        pltpu.VMEM((GDN_V_HEADS, GDN_K_DIM, GDN_V_DIM), F32)],
        input_output_aliases=aliases,
        compiler_params=_cparams("parallel", "arbitrary"),
        name="gdn_scan",
    )(*args)


def _hgrn_kernel(q_ref, f_ref, i_ref, gz_ref, s0_ref, lb_ref, nw_ref,
                 o_ref, sn_ref, st_ref, *, q, nc):
    c = pl.program_id(1)
    nsub = q // HGRN_SUB

    @pl.when(c == 0)
    def _():
        for h in range(HGRN_HEADS):
            st_ref[h] = s0_ref[0, h].T

    lb = lb_ref[...]
    f = lb + (1.0 - lb) * jax.nn.sigmoid(f_ref[...])
    logf = jnp.log(f)
    kk = 1.0 - f
    qs = _silu(q_ref[...].astype(F32))
    row = _iota2((q, q), 0)
    col = _iota2((q, q), 1)
    same = (row // HGRN_SUB) == (col // HGRN_SUB)
    intra = same & (row >= col)
    m_pref = jnp.where(intra, 1.0, 0.0).astype(BF16)
    m_blk = jnp.where(same, 1.0, 0.0).astype(BF16)
    lh, ll = _split(logf)
    gam = _dot(m_pref, lh) + _dot(m_pref, ll)
    gtot = _dot(m_blk, lh) + _dot(m_blk, ll)
    qt = qs * jnp.exp(gam)
    kt = kk * jnp.exp(jnp.minimum(-gam, EXP_CLAMP))
    kd = kk * jnp.exp(gtot - gam)
    edec = jnp.exp(gtot)
    vv = i_ref[...].astype(F32)
    gz = gz_ref[...].astype(F32)
    rsub = _iota2((q, 1), 0) // HGRN_SUB

    heads = range(HGRN_HEADS)
    sls = [slice(h * HGRN_K_DIM, (h + 1) * HGRN_K_DIM) for h in heads]
    qt_b = [qt[:, sl].astype(BF16) for sl in sls]
    v_t = [vv[:, sl].T.astype(BF16) for sl in sls]
    o_in = []
    for h in heads:
        att = jnp.where(intra, _dot_nt(qt_b[h], kt[:, sls[h]].astype(BF16)), 0.0)
        o_in.append(_dot(att.astype(BF16), vv[:, sls[h]].astype(BF16)))

    def window_update(j):
        return [_dot(v_t[h], jnp.where(rsub == j, kd[:, sls[h]], 0.0).astype(BF16)) for h in heads]

    s_t = [st_ref[h] for h in heads]
    o_x = [[] for _ in heads]
    upd_next = window_update(0)
    for j in range(nsub):
        upd = upd_next
        if j + 1 < nsub:
            upd_next = window_update(j + 1)
        r0 = j * HGRN_SUB
        for h in heads:
            o_x[h].append(_dot_nt(qt_b[h][r0:r0 + HGRN_SUB, :], s_t[h].astype(BF16)))
            s_t[h] = s_t[h] * edec[r0:r0 + 1, sls[h]] + upd[h]
    outs = []
    for h in heads:
        st_ref[h] = s_t[h]
        o = o_in[h] + jnp.concatenate(o_x[h], axis=0)
        ms = jnp.mean(o * o, axis=-1, keepdims=True)
        outs.append(o * lax.rsqrt(ms + EPS) * nw_ref[...] * _silu(gz[:, sls[h]]))
    o_ref[...] = jnp.concatenate(outs, axis=1).astype(o_ref.dtype)

    @pl.when(c == nc - 1)
    def _():
        for h in range(HGRN_HEADS):
            sn_ref[0, h] = st_ref[h].T


def hgrn_scan(qp, fz, iv, gz, s0, lb, nw, row0, bsz, seqlen, q, o_prev=None):
    nc = seqlen // q
    blk0 = row0 // q
    assert row0 % q == 0 and seqlen % q == 0 and q % HGRN_SUB == 0
    hk = HGRN_HEADS * HGRN_K_DIM
    hv = HGRN_HEADS * HGRN_V_DIM
    tok = lambda b, c: (blk0 + b * nc + c, 0)
    fixed2 = lambda b, c: (0, 0)
    t = qp.shape[0]
    st_spec = pl.BlockSpec((1, HGRN_HEADS, HGRN_K_DIM, HGRN_V_DIM), lambda b, c: (b, 0, 0, 0))
    in_specs = [pl.BlockSpec((q, hk), tok), pl.BlockSpec((q, hk), tok),
                pl.BlockSpec((q, hv), tok), pl.BlockSpec((q, hv), tok),
                st_spec, pl.BlockSpec((1, hk), fixed2), pl.BlockSpec((1, HGRN_V_DIM), fixed2)]
    args = [qp, fz, iv, gz, s0, lb, nw]
    aliases = {}
    if o_prev is not None:
        in_specs.append(pl.BlockSpec(memory_space=pl.ANY))
        args.append(o_prev)
        aliases = {len(args) - 1: 0}

    def body(*refs):
        if o_prev is not None:
            refs = refs[:7] + refs[8:]
        _hgrn_kernel(*refs, q=q, nc=nc)

    return pl.pallas_call(
        body,
        grid=(bsz, nc),
        in_specs=in_specs,
        out_specs=[pl.BlockSpec((q, hv), tok), st_spec],
        out_shape=[jax.ShapeDtypeStruct((t, hv), BF16),
                   jax.ShapeDtypeStruct((bsz, HGRN_HEADS, HGRN_K_DIM, HGRN_V_DIM), F32)],
        scratch_shapes=[pltpu.VMEM((HGRN_HEADS, HGRN_V_DIM, HGRN_K_DIM), F32)],
        input_output_aliases=aliases,
        compiler_params=_cparams("parallel", "arbitrary"),
        name="hgrn_scan",
    )(*args)


def _out_proj_kernel(*refs, n_in):
    a_refs = refs[:n_in]
    w_refs = refs[n_in:2 * n_in]
    h_ref, o_ref = refs[2 * n_in], refs[2 * n_in + 1]
    acc = h_ref[...]
    for a_ref, w_ref in zip(a_refs, w_refs):
        acc = acc + _dot(a_ref[...], w_ref[...])
    o_ref[...] = acc


def out_proj(acts, ws, h, tm):
    t, d = h.shape
    n_in = len(acts)
    return pl.pallas_call(
        functools.partial(_out_proj_kernel, n_in=n_in),
        grid=(t // tm,),
        in_specs=([pl.BlockSpec((tm, a.shape[1]), lambda i: (i, 0)) for a in acts]
                  + [pl.BlockSpec(w.shape, lambda i: (0, 0)) for w in ws]
                  + [pl.BlockSpec((tm, d), lambda i: (i, 0))]),
        out_specs=pl.BlockSpec((tm, d), lambda i: (i, 0)),
        out_shape=jax.ShapeDtypeStruct((t, d), F32),
        compiler_params=_cparams("parallel"),
        name="out_proj",
    )(*acts, *ws, h)


R_E0, R_E1, R_RANK0, R_RANK1, R_GATE0, R_GATE1 = range(6)
ROUTE_ROWS = 8
ROUTER_ROWS = SUBLANE + MOE_EXPERTS


def _router_kernel(h_ref, g_ref, w_ref, xn_ref, info_ref, cnt_ref, carry_ref, *, tm, nt):
    i = pl.program_id(0)

    @pl.when(i == 0)
    def _():
        carry_ref[...] = jnp.zeros_like(carry_ref)

    x = h_ref[...]
    ms = jnp.mean(x * x, axis=-1, keepdims=True)
    xn = x * lax.rsqrt(ms + EPS) * g_ref[...]
    xn_ref[...] = xn
    logit = _dot3_nt(w_ref[...], xn)
    lg = logit[:MOE_GROUPS, :]
    gmax = jnp.max(lg, axis=0, keepdims=True)
    gidx = _iota2(lg.shape, 0)
    g_sel = jnp.min(jnp.where(lg == gmax, gidx, MOE_GROUPS), axis=0, keepdims=True)
    g_prob = 1.0 / jnp.sum(jnp.exp(lg - gmax), axis=0, keepdims=True)
    el = jnp.zeros((MOE_EPG, tm), F32)
    for g in range(MOE_GROUPS):
        el = jnp.where(g_sel == g, logit[SUBLANE + g * MOE_EPG:SUBLANE + (g + 1) * MOE_EPG, :], el)
    eidx = _iota2(el.shape, 0)
    m1 = jnp.max(el, axis=0, keepdims=True)
    i1 = jnp.min(jnp.where(el == m1, eidx, MOE_EPG), axis=0, keepdims=True)
    el2 = jnp.where(eidx == i1, -jnp.inf, el)
    m2 = jnp.max(el2, axis=0, keepdims=True)
    i2 = jnp.min(jnp.where(el2 == m2, eidx, MOE_EPG), axis=0, keepdims=True)
    ex = jnp.exp(m2 - m1)
    gate0 = g_prob / (1.0 + ex)
    gate1 = g_prob * ex / (1.0 + ex)
    e0 = g_sel * MOE_EPG + i1
    e1 = g_sel * MOE_EPG + i2

    xid = _iota2((MOE_EXPERTS, tm), 0)
    oh0 = xid == e0
    oh1 = xid == e1
    onehot = jnp.where(oh0 | oh1, 1.0, 0.0).astype(BF16)
    upp = jnp.where(_iota2((tm, tm), 0) <= _iota2((tm, tm), 1), 1.0, 0.0).astype(BF16)
    pref = _dot(onehot, upp)
    base = carry_ref[:, 0:1] + pref - 1.0
    rank0 = jnp.sum(jnp.where(oh0, base, 0.0), axis=0, keepdims=True)
    rank1 = jnp.sum(jnp.where(oh1, base, 0.0), axis=0, keepdims=True)
    new_carry = carry_ref[...] + pref[:, tm - 1:tm]
    carry_ref[...] = new_carry

    info_ref[R_E0:R_E0 + 1, :] = e0.astype(F32)
    info_ref[R_E1:R_E1 + 1, :] = e1.astype(F32)
    info_ref[R_RANK0:R_RANK0 + 1, :] = rank0
    info_ref[R_RANK1:R_RANK1 + 1, :] = rank1
    info_ref[R_GATE0:R_GATE0 + 1, :] = gate0
    info_ref[R_GATE1:R_GATE1 + 1, :] = gate1
    info_ref[R_GATE1 + 1:ROUTE_ROWS, :] = jnp.zeros((ROUTE_ROWS - R_GATE1 - 1, tm), F32)

    @pl.when(i == nt - 1)
    def _():
        cnt_ref[...] = new_carry


def moe_router(h, gain, w_rt, tm):
    t, d = h.shape
    nt = t // tm
    return pl.pallas_call(
        functools.partial(_router_kernel, tm=tm, nt=nt),
        grid=(nt,),
        in_specs=[pl.BlockSpec((tm, d), lambda i: (i, 0)),
                  pl.BlockSpec((1, d), lambda i: (0, 0)),
                  pl.BlockSpec((ROUTER_ROWS, d), lambda i: (0, 0))],
        out_specs=[pl.BlockSpec((tm, d), lambda i: (i, 0)),
                   pl.BlockSpec((ROUTE_ROWS, tm), lambda i: (0, i)),
                   pl.BlockSpec((MOE_EXPERTS, LANE), lambda i: (0, 0))],
        out_shape=[jax.ShapeDtypeStruct((t, d), F32),
                   jax.ShapeDtypeStruct((ROUTE_ROWS, t), F32),
                   jax.ShapeDtypeStruct((MOE_EXPERTS, LANE), F32)],
        scratch_shapes=[pltpu.VMEM((MOE_EXPERTS, LANE), F32)],
        compiler_params=_cparams("arbitrary"),
        name="moe_router",
    )(h, gain.reshape(1, d), w_rt)


def _row_dma_loops(row_copy, tm):
    def each(fn):
        def body(g, carry):
            base = pl.multiple_of(g * SUBLANE, SUBLANE)
            for r in range(SUBLANE):
                fn(base + r)
            return carry
        return lambda: lax.fori_loop(0, tm // SUBLANE, body, 0)

    def start(row):
        row_copy(row, 0).start(priority=0)
        row_copy(row, 1).start(priority=1)

    def wait(row):
        row_copy(row, 0).wait()
        row_copy(row, 1).wait()

    return each(start), each(wait)


def _dispatch_kernel(pad_end_ref, own_end_ref, dest_ref, xn_ref, *rest, tm, first):
    buf_ref, zero_ref, sem, zsem = rest[-4:]

    def zero_padding_tiles():
        zero_ref[...] = jnp.zeros_like(zero_ref)

        def tile_copy(j):
            return pltpu.make_async_copy(
                zero_ref, buf_ref.at[pl.ds(pl.multiple_of(j * MOE_ROW_TILE, MOE_ROW_TILE), MOE_ROW_TILE)], zsem)

        spans = [(own_end_ref[e] // MOE_ROW_TILE, pad_end_ref[e] // MOE_ROW_TILE) for e in range(MOE_EXPERTS)]
        spans.append((pad_end_ref[MOE_EXPERTS - 1] // MOE_ROW_TILE, buf_ref.shape[0] // MOE_ROW_TILE))
        for lo, hi in spans:
            lax.fori_loop(lo, hi, lambda j, c: (tile_copy(j).start(), c)[1], 0)
        for lo, hi in spans:
            lax.fori_loop(lo, hi, lambda j, c: (tile_copy(j).wait(), c)[1], 0)

    if first:
        pl.when(pl.program_id(0) == 0)(zero_padding_tiles)

    def row_copy(row, k):
        return pltpu.make_async_copy(xn_ref.at[pl.ds(row, 1)],
                                     buf_ref.at[pl.ds(dest_ref[0, 0, k * tm + row], 1)], sem)

    start_all, wait_all = _row_dma_loops(row_copy, tm)
    start_all()
    wait_all()


def moe_dispatch(xn, dest3, pad_end, own_end, n_rows, tm, buf_prev=None):
    t, d = xn.shape
    in_specs = [pl.BlockSpec((1, 1, 2 * tm), lambda i, pe, pd: (i, 0, 0), memory_space=pltpu.SMEM),
                pl.BlockSpec((tm, d), lambda i, pe, pd: (i, 0))]
    args = [pad_end, own_end, dest3, xn]
    aliases = {}
    if buf_prev is not None:
        in_specs.append(pl.BlockSpec(memory_space=pl.ANY))
        args.append(buf_prev)
        aliases = {len(args) - 1: 0}
    return pl.pallas_call(
        functools.partial(_dispatch_kernel, tm=tm, first=buf_prev is None),
        grid_spec=pltpu.PrefetchScalarGridSpec(
            num_scalar_prefetch=2,
            grid=(t // tm,),
            in_specs=in_specs,
            out_specs=pl.BlockSpec(memory_space=pl.ANY),
            scratch_shapes=[pltpu.VMEM((MOE_ROW_TILE, d), xn.dtype),
                            pltpu.SemaphoreType.DMA(()), pltpu.SemaphoreType.DMA(())],
        ),
        out_shape=jax.ShapeDtypeStruct((n_rows, d), xn.dtype),
        input_output_aliases=aliases,
        compiler_params=_cparams("arbitrary"),
        name="moe_dispatch",
    )(*args)


def _expert_kernel(te_ref, nu_ref, x_ref, wg_ref, wu_ref, wd_ref, y_ref, wg_b, wu_b, wd_b):
    i = pl.program_id(0)

    @pl.when((i == 0) | (te_ref[i] != te_ref[jnp.maximum(i - 1, 0)]))
    def _():
        wg_b[...] = wg_ref[0].astype(BF16)
        wu_b[...] = wu_ref[0].astype(BF16)
        wd_b[...] = wd_ref[0].astype(BF16)

    @pl.when(i < nu_ref[0])
    def _():
        x = x_ref[...].astype(BF16)
        hid = _silu(_dot(x, wg_b[...])) * _dot(x, wu_b[...])
        y_ref[...] = _dot(hid.astype(BF16), wd_b[...])

    @pl.when(i >= nu_ref[0])
    def _():
        y_ref[...] = jnp.zeros_like(y_ref)


def moe_experts(buf, tile_e, n_used, wg, wu, wd):
    n_rows, d = buf.shape
    n_tiles = n_rows // MOE_ROW_TILE
    return pl.pallas_call(
        _expert_kernel,
        grid_spec=pltpu.PrefetchScalarGridSpec(
            num_scalar_prefetch=2,
            grid=(n_tiles,),
            in_specs=[pl.BlockSpec((MOE_ROW_TILE, d), lambda i, te, nu: (jnp.minimum(i, nu[0] - 1), 0)),
                      pl.BlockSpec((1, d, MOE_D_EXPERT), lambda i, te, nu: (te[i], 0, 0)),
                      pl.BlockSpec((1, d, MOE_D_EXPERT), lambda i, te, nu: (te[i], 0, 0)),
                      pl.BlockSpec((1, MOE_D_EXPERT, d), lambda i, te, nu: (te[i], 0, 0))],
            out_specs=pl.BlockSpec((MOE_ROW_TILE, d), lambda i, te, nu: (i, 0)),
            scratch_shapes=[pltpu.VMEM((d, MOE_D_EXPERT), BF16), pltpu.VMEM((d, MOE_D_EXPERT), BF16),
                            pltpu.VMEM((MOE_D_EXPERT, d), BF16)],
        ),
        out_shape=jax.ShapeDtypeStruct((n_rows, d), F32),
        compiler_params=_cparams("arbitrary"),
        name="moe_experts",
    )(tile_e, n_used, buf, wg, wu, wd)


def _combine_kernel(dest_ref, h_ref, info_ref, fg_ref, yb_ref, o_ref, ybuf, sem, *, tm, final_norm):
    def row_copy(row, k):
        return pltpu.make_async_copy(yb_ref.at[pl.ds(dest_ref[0, 0, k * tm + row], 1)],
                                     ybuf.at[k, pl.ds(row, 1)], sem)

    start_all, wait_all = _row_dma_loops(row_copy, tm)
    start_all()
    info_t = info_ref[...].T
    g0 = info_t[:, R_GATE0:R_GATE0 + 1]
    g1 = info_t[:, R_GATE1:R_GATE1 + 1]
    wait_all()
    y = h_ref[...] + (g0 * ybuf[0] + g1 * ybuf[1])
    if final_norm:
        ms = jnp.mean(y * y, axis=-1, keepdims=True)
        y = y * lax.rsqrt(ms + EPS) * fg_ref[...]
    o_ref[...] = y


def moe_combine(h, info, dest3, yb, final_gain, final_norm, tm):
    t, d = h.shape
    return pl.pallas_call(
        functools.partial(_combine_kernel, tm=tm, final_norm=final_norm),
        grid=(t // tm,),
        in_specs=[pl.BlockSpec((1, 1, 2 * tm), lambda i: (i, 0, 0), memory_space=pltpu.SMEM),
                  pl.BlockSpec((tm, d), lambda i: (i, 0)),
                  pl.BlockSpec((ROUTE_ROWS, tm), lambda i: (0, i)),
                  pl.BlockSpec((1, d), lambda i: (0, 0)),
                  pl.BlockSpec(memory_space=pl.ANY)],
        out_specs=pl.BlockSpec((tm, d), lambda i: (i, 0)),
        out_shape=jax.ShapeDtypeStruct((t, d), F32),
        scratch_shapes=[pltpu.VMEM((2, tm, d), F32), pltpu.SemaphoreType.DMA(())],
        compiler_params=_cparams("arbitrary"),
        name="moe_combine",
    )(dest3, h, info, final_gain.reshape(1, d), yb)


def hier_moe_residual(hs, tms, gain, w_rt, wg, wu, wd, final_gain, final_norm):
    d = hs[0].shape[1]
    routed = [moe_router(h, gain, w_rt, tm) for h, tm in zip(hs, tms)]
    counts = [cnt[:, 0].astype(jnp.int32) for _, _, cnt in routed]
    total = sum(counts)
    padded = (total + MOE_ROW_TILE - 1) // MOE_ROW_TILE * MOE_ROW_TILE
    pad_end = jnp.cumsum(padded).astype(jnp.int32)
    pad_start = pad_end - padded
    n_tiles = 2 * sum(h.shape[0] for h in hs) // MOE_ROW_TILE + MOE_EXPERTS
    n_used = (pad_end[-1] // MOE_ROW_TILE).astype(jnp.int32)
    tile_row = jnp.minimum(jnp.arange(n_tiles, dtype=jnp.int32), n_used - 1) * MOE_ROW_TILE
    tile_e = jnp.sum(tile_row[:, None] >= pad_end[None, :], axis=1).astype(jnp.int32)
    buf = None
    dests = []
    first_row = pad_start
    for (xn, info, _), cnt, tm in zip(routed, counts, tms):
        t = xn.shape[0]
        e01 = info[R_E0:R_E1 + 1].astype(jnp.int32)
        rank01 = info[R_RANK0:R_RANK1 + 1].astype(jnp.int32)
        onehot = e01[:, :, None] == jnp.arange(MOE_EXPERTS, dtype=jnp.int32)
        dest = rank01 + jnp.sum(jnp.where(onehot, first_row, 0), axis=-1)
        dest3 = dest.reshape(2, t // tm, tm).transpose(1, 0, 2).reshape(t // tm, 1, 2 * tm)
        buf = moe_dispatch(xn, dest3, pad_end, first_row + cnt, n_tiles * MOE_ROW_TILE, tm, buf)
        dests.append(dest3)
        first_row = first_row + cnt
    yb = moe_experts(buf, tile_e, n_used.reshape(1), wg, wu, wd)
    return [moe_combine(h, info, dest3, yb, final_gain, final_norm, tm)
            for h, (_, info, _), dest3, tm in zip(hs, routed, dests, tms)]


def _cast_split_kernel(w_ref, *o_refs, pieces):
    w = w_ref[...]
    for o_ref, cols in zip(o_refs, pieces):
        parts = [w[:, a:b] for a, b in cols]
        width = sum(b - a for a, b in cols)
        if width < o_ref.shape[1]:
            parts.append(jnp.zeros((w.shape[0], o_ref.shape[1] - width), F32))
        o_ref[...] = (parts[0] if len(parts) == 1 else jnp.concatenate(parts, axis=1)).astype(BF16)


def cast_split(w, pieces, widths, tk, name):
    k, n = w.shape
    assert k % tk == 0
    return pl.pallas_call(
        functools.partial(_cast_split_kernel, pieces=pieces),
        grid=(k // tk,),
        in_specs=[pl.BlockSpec((tk, n), lambda i: (i, 0))],
        out_specs=[pl.BlockSpec((tk, wd), lambda i: (i, 0)) for wd in widths],
        out_shape=[jax.ShapeDtypeStruct((k, wd), BF16) for wd in widths],
        compiler_params=_cparams("parallel"),
        name=name,
    )(w)


def _prep_ab_weights(w_in_ab):
    o = [int(v) for v in np.cumsum([0, SSD_INNER, SSD_CONV_DIM, SSD_HEADS, GDN_CONV_DIM, GDN_VD,
                                    GDN_V_HEADS, GDN_V_HEADS])]
    seg = lambda k: (o[k], o[k + 1])
    pieces = ((seg(0),), (seg(1),), (seg(3),), (seg(4),), (seg(2), seg(5), seg(6)))
    widths = (SSD_INNER, SSD_CONV_DIM, GDN_CONV_DIM, GDN_VD, SMALL_W)
    return cast_split(w_in_ab, pieces, widths, 128, "prep_w_in_ab")


def _prep_c_weights(w_in_c):
    pieces = tuple(((k * D_MODEL, (k + 1) * D_MODEL),) for k in range(4))
    return cast_split(w_in_c, pieces, (D_MODEL,) * 4, 128, "prep_w_in_c")


def _ab_layer(h, states, w, g):
    st_ssd, st_ssd_conv, st_gdn, st_gdn_conv, _ = states
    zs, xbc, qkv, zg, small = norm_proj(h, w['norm_mix'][0], w['w_in_ab'], (BF16, BF16, BF16, BF16, F32),
                                        g['tm'], "in_proj_ab")
    y, n_ssd, n_ssd_conv = ssd_scan(xbc, small, zs, st_ssd[:, 0], st_ssd_conv[:, 0], w, 0,
                                    g['bsz'], g['seqlen'], g['q_ssd'])
    o, n_gdn, n_gdn_conv = gdn_scan(qkv, small, zg, st_gdn[:, 0], st_gdn_conv[:, 0], w, 0,
                                    g['bsz'], g['seqlen'], g['q_gdn'])
    h = out_proj([y, o], [w['w_out_ab'][:SSD_INNER], w['w_out_ab'][SSD_INNER:]], h, g['tm'])
    return h, (n_ssd[:, None], n_ssd_conv[:, None], n_gdn[:, None], n_gdn_conv[:, None])


def _c_layer(h, states, w, g):
    qp, fz, iv, gz = norm_proj(h, w['norm_mix'][1], w['w_in_c'], (BF16, F32, BF16, BF16), g['tm'], "in_proj_c")
    oc, n_hgrn = hgrn_scan(qp, fz, iv, gz, states[4][:, 0], w['hgrn_lb'], w['hgrn_norm'], 0,
                           g['bsz'], g['seqlen'], g['q_hgrn'])
    return out_proj([oc], [w['w_out_c']], h, g['tm']), n_hgrn[:, None]


def _moe_layer(hs, groups, w, li, final_norm):
    return hier_moe_residual(hs, [g['tm'] for g in groups], w['norm_ffn'][li], w['moe_w_rt'][li],
                             w['moe_w_gate'][li], w['moe_w_up'][li], w['moe_w_down'][li],
                             w['norm_final'], final_norm)


def kernel(x_prompt, x_sample, state_ssd, state_ssd_conv, state_gdn, state_gdn_conv, state_hgrn,
           norm_mix, norm_ffn, norm_final,
           w_in_ab, ssd_conv_w, ssd_conv_b, ssd_dt_bias, ssd_a_log, ssd_d, ssd_norm,
           gdn_conv_w, gdn_dt_bias, gdn_a_log, gdn_norm, w_out_ab,
           w_in_c, hgrn_lb_logits, hgrn_norm, w_out_c,
           moe_w_group, moe_w_router, moe_w_gate, moe_w_up, moe_w_down):
    depth = norm_mix.shape[0]
    sm = jax.nn.softmax(hgrn_lb_logits.astype(F32), axis=0)
    lower_bounds = jnp.cumsum(sm, axis=0) - sm[0]
    w_rt = jnp.concatenate([jnp.swapaxes(moe_w_group, 1, 2),
                            jnp.zeros((depth, SUBLANE - MOE_GROUPS, D_MODEL), F32),
                            jnp.swapaxes(moe_w_router, 1, 2)], axis=1)
    expand = (jnp.arange(SSD_INNER)[None, :] // SSD_HEAD_DIM == jnp.arange(SSD_HEADS)[:, None]).astype(F32)
    w = dict(
        norm_mix=norm_mix, norm_ffn=norm_ffn, norm_final=norm_final,
        w_in_ab=_prep_ab_weights(w_in_ab[0]),
        ssd_conv_w=ssd_conv_w[0], ssd_conv_b=ssd_conv_b[0][None, :], ssd_dt_bias=ssd_dt_bias[0][None, :],
        ssd_a_log=ssd_a_log[0][None, :], ssd_d_full=jnp.repeat(ssd_d[0], SSD_HEAD_DIM)[None, :],
        ssd_norm=ssd_norm[0][None, :], ssd_expand=expand,
        gdn_conv_w=gdn_conv_w[0], gdn_dt_bias=gdn_dt_bias[0][None, :], gdn_a_log=gdn_a_log[0][None, :],
        gdn_norm=gdn_norm[0][None, :], w_out_ab=w_out_ab[0].astype(BF16),
        w_in_c=_prep_c_weights(w_in_c[0]),
        hgrn_lb=lower_bounds[1][None, :], hgrn_norm=hgrn_norm[0][None, :],
        w_out_c=w_out_c[0].astype(BF16), moe_w_rt=w_rt,
        moe_w_gate=moe_w_gate, moe_w_up=moe_w_up, moe_w_down=moe_w_down,
    )
    bp, lp, _ = x_prompt.shape
    bs, ls, _ = x_sample.shape
    zeros = lambda ref: jnp.zeros((bp,) + ref.shape[1:], x_prompt.dtype)
    p_states = tuple(zeros(s) for s in (state_ssd, state_ssd_conv, state_gdn, state_gdn_conv, state_hgrn))
    s_states = (state_ssd, state_ssd_conv, state_gdn, state_gdn_conv, state_hgrn)
    groups = [dict(bsz=bp, seqlen=lp, q_ssd=256, q_gdn=64, q_hgrn=128, tm=512),
              dict(bsz=bs, seqlen=ls, q_ssd=ls, q_gdn=ls, q_hgrn=ls, tm=bs * ls)]
    states = [p_states, s_states]
    hs = [x_prompt.reshape(bp * lp, D_MODEL), x_sample.reshape(bs * ls, D_MODEL)]
    ab = [_ab_layer(h, st, w, g) for h, st, g in zip(hs, states, groups)]
    hs = _moe_layer([a[0] for a in ab], groups, w, 0, False)
    cl = [_c_layer(h, st, w, g) for h, st, g in zip(hs, states, groups)]
    ys = _moe_layer([c[0] for c in cl], groups, w, 1, True)
    y_p, y_s = (y.reshape(g['bsz'], g['seqlen'], D_MODEL) for y, g in zip(ys, groups))
    return (y_p, y_s) + ab[0][1] + (cl[0][1],) + ab[1][1] + (cl[1][1],)
```

```python
import functools

import jax
import jax.numpy as jnp
import numpy as np
from jax import lax
from jax.experimental import pallas as pl
from jax.experimental.pallas import tpu as pltpu

F32 = jnp.float32
BF16 = jnp.bfloat16

D_MODEL = 1024
EPS = 1e-6
CONV_W = 4
CONV_TAIL = CONV_W - 1
SSD_HEADS = 16
SSD_HEAD_DIM = 64
SSD_INNER = SSD_HEADS * SSD_HEAD_DIM
SSD_GROUPS = 2
SSD_STATE = 128
SSD_CONV_DIM = SSD_INNER + 2 * SSD_GROUPS * SSD_STATE
GDN_K_HEADS = 4
GDN_V_HEADS = 8
GDN_K_DIM = 128
GDN_V_DIM = 128
GDN_QK = GDN_K_HEADS * GDN_K_DIM
GDN_VD = GDN_V_HEADS * GDN_V_DIM
GDN_CONV_DIM = 2 * GDN_QK + GDN_VD
HGRN_HEADS = 8
HGRN_K_DIM = 128
HGRN_V_DIM = 128
HGRN_SUB = 32
MOE_GROUPS = 4
MOE_EPG = 8
MOE_EXPERTS = MOE_GROUPS * MOE_EPG
MOE_D_EXPERT = 512
MOE_ROW_TILE = 256
SMALL_W = 128
EXP_CLAMP = 80.0

LANE = 128
SUBLANE = 8
VMEM_LIMIT = 56 * 1024 * 1024


def _cparams(*sem):
    return pltpu.CompilerParams(dimension_semantics=sem, vmem_limit_bytes=VMEM_LIMIT)


def _dot(a, b):
    return jnp.dot(a, b, preferred_element_type=F32)


def _dot_nt(a, b):
    return lax.dot_general(a, b, (((1,), (1,)), ((), ())), preferred_element_type=F32)


def _split(x):
    hi = x.astype(BF16)
    lo = (x - hi.astype(F32)).astype(BF16)
    return hi, lo


def _dot3(a, b):
    ah, al = _split(a)
    bh, bl = _split(b)
    return _dot(ah, bh) + (_dot(ah, bl) + _dot(al, bh))


def _dot3_nt(a, b):
    ah, al = _split(a)
    bh, bl = _split(b)
    return _dot_nt(ah, bh) + (_dot_nt(ah, bl) + _dot_nt(al, bh))


def _dot2_exact_lhs(m_bf16, x):
    xh, xl = _split(x)
    return _dot(m_bf16, xh) + _dot(m_bf16, xl)


def _silu(x):
    return x * jax.nn.sigmoid(x)


def _softplus(x):
    return jnp.maximum(x, 0.0) + jnp.log1p(jnp.exp(-jnp.abs(x)))


def _iota2(shape, dim):
    return lax.broadcasted_iota(jnp.int32, shape, dim)


def _norm_proj_kernel(*refs, n_out):
    x_ref, g_ref = refs[:2]
    w_refs = refs[2:2 + n_out]
    o_refs = refs[2 + n_out:]
    x = x_ref[...]
    ms = jnp.mean(x * x, axis=-1, keepdims=True)
    u = (x * lax.rsqrt(ms + EPS) * g_ref[...]).astype(BF16)
    for w_ref, o_ref in zip(w_refs, o_refs):
        o_ref[...] = _dot(u, w_ref[...]).astype(o_ref.dtype)


def norm_proj(x, gain, ws, dtypes, tm, name):
    t, d = x.shape
    assert t % tm == 0
    return pl.pallas_call(
        functools.partial(_norm_proj_kernel, n_out=len(ws)),
        grid=(t // tm,),
        in_specs=([pl.BlockSpec((tm, d), lambda i: (i, 0)), pl.BlockSpec((1, d), lambda i: (0, 0))]
                  + [pl.BlockSpec(w.shape, lambda i: (0, 0)) for w in ws]),
        out_specs=[pl.BlockSpec((tm, w.shape[1]), lambda i: (i, 0)) for w in ws],
        out_shape=[jax.ShapeDtypeStruct((t, w.shape[1]), dt) for w, dt in zip(ws, dtypes)],
        compiler_params=_cparams("parallel"),
        name=name,
    )(x, gain.reshape(1, d), *ws)


def _conv_chunk(x_ref, c0_ref, cn_ref, ext_ref, w_ref, first, last, q):
    @pl.when(first)
    def _():
        ext_ref[SUBLANE - CONV_TAIL:SUBLANE, :] = c0_ref[0]

    ext_ref[SUBLANE:SUBLANE + q, :] = x_ref[...].astype(F32)
    acc = ext_ref[SUBLANE:SUBLANE + q, :] * w_ref[CONV_TAIL:CONV_W, :]
    for j in range(CONV_TAIL):
        lo = SUBLANE - CONV_TAIL + j
        acc = acc + ext_ref[lo:lo + q, :] * w_ref[j:j + 1, :]
    tail = ext_ref[SUBLANE + q - CONV_TAIL:SUBLANE + q, :]

    @pl.when(last)
    def _():
        cn_ref[0] = tail

    ext_ref[SUBLANE - CONV_TAIL:SUBLANE, :] = tail
    return acc


def _ssd_kernel(xbc_ref, sm_ref, z_ref, s0_ref, c0_ref, cw_ref, cb_ref, dtb_ref, dtbt_ref,
                alog_ref, alogt_ref, dfull_ref, nw_ref, exp_ref,
                y_ref, sn_ref, cn_ref, ext_ref, st_ref, *, q, nc):
    c = pl.program_id(1)
    first = c == 0
    last = c == nc - 1
    hp = SSD_HEADS // SSD_GROUPS * SSD_HEAD_DIM
    npair = SSD_HEADS // 2

    @pl.when(first)
    def _():
        for pr in range(npair):
            g, j = divmod(pr, npair // SSD_GROUPS)
            st_ref[g, :, j * LANE:(j + 1) * LANE] = s0_ref[0, pr].T

    xbc = _conv_chunk(xbc_ref, c0_ref, cn_ref, ext_ref, cw_ref, first, last, q) + cb_ref[...]
    xbc = _silu(xbc)
    xs = xbc[:, :SSD_INNER]
    xs_b = xs.astype(BF16)

    sm = sm_ref[...]
    sm_t = sm.T
    dt = _softplus(sm[:, :SSD_HEADS] + dtb_ref[...])
    dt_t = _softplus(sm_t[:SSD_HEADS, :] + dtbt_ref[...])
    a = -jnp.exp(alog_ref[...]) * dt
    a_t = -jnp.exp(alogt_ref[...]) * dt_t
    row = _iota2((q, q), 0)
    col = _iota2((q, q), 1)
    causal = row >= col
    low = jnp.where(causal, 1.0, 0.0).astype(BF16)
    upp = jnp.where(row <= col, 1.0, 0.0).astype(BF16)
    cum = _dot2_exact_lhs(low, a)
    ah, al = _split(a_t)
    cum_t = _dot(ah, upp) + _dot(al, upp)
    cum_last = cum[q - 1:q, :]
    ecum = jnp.exp(cum)
    tail = jnp.exp(cum_last - cum) * dt
    ecum_f = _dot3(ecum, exp_ref[...])
    tail_f = _dot3(tail, exp_ref[...])
    dlast_f = _dot3(jnp.exp(cum_last), exp_ref[...])

    lane = _iota2((q, LANE), 1)
    y_parts = []
    for g in range(SSD_GROUPS):
        bm = xbc[:, SSD_INNER + g * SSD_STATE:SSD_INNER + (g + 1) * SSD_STATE]
        cm = xbc[:, SSD_INNER + (SSD_GROUPS + g) * SSD_STATE:SSD_INNER + (SSD_GROUPS + g + 1) * SSD_STATE]
        bm_b = bm.astype(BF16)
        cm_b = cm.astype(BF16)
        gmat = _dot_nt(cm_b, bm_b)
        st = st_ref[g]
        y_int = _dot(cm_b, st.astype(BF16)) * ecum_f[:, g * hp:(g + 1) * hp]
        for j in range(npair // SSD_GROUPS):
            pr = g * (npair // SSD_GROUPS) + j
            xp = xs_b[:, pr * LANE:(pr + 1) * LANE]
            ys = []
            for hh in range(2):
                h = 2 * pr + hh
                diff = cum[:, h:h + 1] - cum_t[h:h + 1, :]
                dec = jnp.exp(jnp.where(causal, diff, -jnp.inf)) * dt_t[h:h + 1, :]
                ys.append(_dot((gmat * dec).astype(BF16), xp))
            y_parts.append(jnp.where(lane < SSD_HEAD_DIM, ys[0], ys[1]) + y_int[:, j * LANE:(j + 1) * LANE])
        xsc = (xs[:, g * hp:(g + 1) * hp] * tail_f[:, g * hp:(g + 1) * hp]).astype(BF16)
        st_ref[g] = st * dlast_f[:, g * hp:(g + 1) * hp] + _dot(bm.T.astype(BF16), xsc)

    y = jnp.concatenate(y_parts, axis=1) + dfull_ref[...] * xs
    y = y * _silu(z_ref[...].astype(F32))
    outs = []
    for g in range(SSD_GROUPS):
        yg = y[:, g * hp:(g + 1) * hp]
        ms = jnp.mean(yg * yg, axis=-1, keepdims=True)
        outs.append(yg * lax.rsqrt(ms + EPS))
    y_ref[...] = (jnp.concatenate(outs, axis=1) * nw_ref[...]).astype(y_ref.dtype)

    @pl.when(last)
    def _():
        for pr in range(npair):
            g, j = divmod(pr, npair // SSD_GROUPS)
            sn_ref[0, pr] = st_ref[g, :, j * LANE:(j + 1) * LANE].T


def ssd_scan(xbc, small, z, s0, c0, p, row0, bsz, seqlen, q, y_prev=None):
    nc = seqlen // q
    blk0 = row0 // q
    assert row0 % q == 0 and seqlen % q == 0
    npair = SSD_HEADS // 2
    tok = lambda b, c: (blk0 + b * nc + c, 0)
    fixed2 = lambda b, c: (0, 0)
    t = xbc.shape[0]
    in_specs = [
        pl.BlockSpec((q, SSD_CONV_DIM), tok),
        pl.BlockSpec((q, SMALL_W), tok),
        pl.BlockSpec((q, SSD_INNER), tok),
        pl.BlockSpec((1, npair, LANE, SSD_STATE), lambda b, c: (b, 0, 0, 0)),
        pl.BlockSpec((1, CONV_TAIL, SSD_CONV_DIM), lambda b, c: (b, 0, 0)),
        pl.BlockSpec((CONV_W, SSD_CONV_DIM), fixed2),
        pl.BlockSpec((1, SSD_CONV_DIM), fixed2),
        pl.BlockSpec((1, SSD_HEADS), fixed2),
        pl.BlockSpec((SSD_HEADS, 1), fixed2),
        pl.BlockSpec((1, SSD_HEADS), fixed2),
        pl.BlockSpec((SSD_HEADS, 1), fixed2),
        pl.BlockSpec((1, SSD_INNER), fixed2),
        pl.BlockSpec((1, SSD_INNER), fixed2),
        pl.BlockSpec((SSD_HEADS, SSD_INNER), fixed2),
    ]
    args = [xbc, small, z, s0.reshape(bsz, npair, LANE, SSD_STATE), c0,
            p['ssd_conv_w'], p['ssd_conv_b'], p['ssd_dt_bias'], p['ssd_dt_bias'].reshape(-1, 1),
            p['ssd_a_log'], p['ssd_a_log'].reshape(-1, 1), p['ssd_d_full'], p['ssd_norm'], p['ssd_expand']]
    aliases = {}
    if y_prev is not None:
        in_specs.append(pl.BlockSpec(memory_space=pl.ANY))
        args.append(y_prev)
        aliases = {len(args) - 1: 0}

    def body(*refs):
        if y_prev is not None:
            refs = refs[:14] + refs[15:]
        _ssd_kernel(*refs, q=q, nc=nc)

    y, sn, cn = pl.pallas_call(
        body,
        grid=(bsz, nc),
        in_specs=in_specs,
        out_specs=[pl.BlockSpec((q, SSD_INNER), tok),
                   pl.BlockSpec((1, npair, LANE, SSD_STATE), lambda b, c: (b, 0, 0, 0)),
                   pl.BlockSpec((1, CONV_TAIL, SSD_CONV_DIM), lambda b, c: (b, 0, 0))],
        out_shape=[jax.ShapeDtypeStruct((t, SSD_INNER), BF16),
                   jax.ShapeDtypeStruct((bsz, npair, LANE, SSD_STATE), F32),
                   jax.ShapeDtypeStruct((bsz, CONV_TAIL, SSD_CONV_DIM), F32)],
        scratch_shapes=[pltpu.VMEM((q + SUBLANE, SSD_CONV_DIM), F32),
                        pltpu.VMEM((SSD_GROUPS, SSD_STATE, SSD_INNER // SSD_GROUPS), F32)],
        input_output_aliases=aliases,
        compiler_params=_cparams("parallel", "arbitrary"),
        name="ssd_scan",
    )(*args)
    return y, sn.reshape(bsz, SSD_HEADS, SSD_HEAD_DIM, SSD_STATE), cn


def _solve_unit_lower(amats, rhss, q):
    ps = [-a for a in amats]
    ys = list(rhss)
    span = 1
    while span < q:
        last = 2 * span >= q
        nxt_p, nxt_y = [], []
        for p, y in zip(ps, ys):
            ph, plo = _split(p)
            yh, ylo = _split(y)
            if last:
                rh, rl = yh, ylo
            else:
                rh = jnp.concatenate([yh, ph], axis=1)
                rl = jnp.concatenate([ylo, plo], axis=1)
            z = _dot(ph, rh) + (_dot(ph, rl) + _dot(plo, rh))
            w = y.shape[1]
            nxt_y.append(y + z[:, :w])
            nxt_p.append(None if last else z[:, w:])
        ps, ys = nxt_p, nxt_y
        span *= 2
    return ys


def _gdn_kernel(qkv_ref, sm_ref, z_ref, s0_ref, c0_ref, cw_ref, dtb_ref, dtbt_ref,
                alog_ref, alogt_ref, nw_ref,
                o_ref, sn_ref, cn_ref, ext_ref, st_ref, *, q, nc):
    c = pl.program_id(1)
    first = c == 0
    last = c == nc - 1
    rep = GDN_V_HEADS // GDN_K_HEADS
    b_off = SSD_HEADS
    a_off = SSD_HEADS + GDN_V_HEADS

    @pl.when(first)
    def _():
        st_ref[...] = s0_ref[0]

    qkv = _silu(_conv_chunk(qkv_ref, c0_ref, cn_ref, ext_ref, cw_ref, first, last, q))
    sm = sm_ref[...]
    sm_t = sm.T
    beta = jax.nn.sigmoid(sm[:, b_off:b_off + GDN_V_HEADS])
    g = -jnp.exp(alog_ref[...]) * _softplus(sm[:, a_off:a_off + GDN_V_HEADS] + dtb_ref[...])
    g_t = -jnp.exp(alogt_ref[...]) * _softplus(sm_t[a_off:a_off + GDN_V_HEADS, :] + dtbt_ref[...])
    row = _iota2((q, q), 0)
    col = _iota2((q, q), 1)
    causal = row >= col
    strict = row > col
    low = jnp.where(causal, 1.0, 0.0).astype(BF16)
    upp = jnp.where(row <= col, 1.0, 0.0).astype(BF16)
    gam = _dot2_exact_lhs(low, g)
    gh, gl = _split(g_t)
    gam_t = _dot(gh, upp) + _dot(gl, upp)
    gam_last = gam[q - 1:q, :]
    eg = jnp.exp(gam)
    tail = jnp.exp(gam_last - gam)
    elast = jnp.exp(gam_last)

    qk_l, kk_l, kn_l, qk_rows = [], [], [], []
    for kh in range(GDN_K_HEADS):
        qh = qkv[:, kh * GDN_K_DIM:(kh + 1) * GDN_K_DIM]
        kx = qkv[:, GDN_QK + kh * GDN_K_DIM:GDN_QK + (kh + 1) * GDN_K_DIM]
        qn = qh * lax.rsqrt(jnp.sum(qh * qh, axis=-1, keepdims=True) + EPS) * (GDN_K_DIM ** -0.5)
        kn = kx * lax.rsqrt(jnp.sum(kx * kx, axis=-1, keepdims=True) + EPS)
        both = jnp.concatenate([qn.astype(BF16), kn.astype(BF16)], axis=0)
        prod = _dot_nt(both, both[q:])
        qk_l.append(prod[:q])
        kk_l.append(prod[q:])
        kn_l.append(kn)
        qk_rows.append(both)
    s_l = [st_ref[h] for h in range(GDN_V_HEADS)]
    qs_ks = [_dot(qk_rows[h // rep], s_l[h].astype(BF16)) for h in range(GDN_V_HEADS)]
    amats, rhss, decs = [], [], []
    for h in range(GDN_V_HEADS):
        v = qkv[:, 2 * GDN_QK + h * GDN_V_DIM:2 * GDN_QK + (h + 1) * GDN_V_DIM]
        diff = gam[:, h:h + 1] - gam_t[h:h + 1, :]
        dec = jnp.exp(jnp.where(causal, diff, -jnp.inf))
        bh = beta[:, h:h + 1]
        amats.append(jnp.where(strict, bh * dec * kk_l[h // rep], 0.0))
        rhss.append(bh * (v - eg[:, h:h + 1] * qs_ks[h][q:]))
        decs.append(dec)
    us = _solve_unit_lower(amats, rhss, q)
    z = z_ref[...].astype(F32)
    outs = []
    for h in range(GDN_V_HEADS):
        u_b = us[h].astype(BF16)
        o = _dot((qk_l[h // rep] * decs[h]).astype(BF16), u_b) + eg[:, h:h + 1] * qs_ks[h][:q]
        kt = (kn_l[h // rep] * tail[:, h:h + 1]).T.astype(BF16)
        st_ref[h] = s_l[h] * elast[:, h:h + 1] + _dot(kt, u_b)
        ms = jnp.mean(o * o, axis=-1, keepdims=True)
        zh = z[:, h * GDN_V_DIM:(h + 1) * GDN_V_DIM]
        outs.append(o * lax.rsqrt(ms + EPS) * nw_ref[...] * _silu(zh))
    o_ref[...] = jnp.concatenate(outs, axis=1).astype(o_ref.dtype)

    @pl.when(last)
    def _():
        sn_ref[0] = st_ref[...]


def gdn_scan(qkv, small, z, s0, c0, p, row0, bsz, seqlen, q, o_prev=None):
    nc = seqlen // q
    blk0 = row0 // q
    assert row0 % q == 0 and seqlen % q == 0
    tok = lambda b, c: (blk0 + b * nc + c, 0)
    fixed2 = lambda b, c: (0, 0)
    t = qkv.shape[0]
    st_spec = pl.BlockSpec((1, GDN_V_HEADS, GDN_K_DIM, GDN_V_DIM), lambda b, c: (b, 0, 0, 0))
    cv_spec = pl.BlockSpec((1, CONV_TAIL, GDN_CONV_DIM), lambda b, c: (b, 0, 0))
    in_specs = [
        pl.BlockSpec((q, GDN_CONV_DIM), tok),
        pl.BlockSpec((q, SMALL_W), tok),
        pl.BlockSpec((q, GDN_VD), tok),
        st_spec, cv_spec,
        pl.BlockSpec((CONV_W, GDN_CONV_DIM), fixed2),
        pl.BlockSpec((1, GDN_V_HEADS), fixed2),
        pl.BlockSpec((GDN_V_HEADS, 1), fixed2),
        pl.BlockSpec((1, GDN_V_HEADS), fixed2),
        pl.BlockSpec((GDN_V_HEADS, 1), fixed2),
        pl.BlockSpec((1, GDN_V_DIM), fixed2),
    ]
    args = [qkv, small, z, s0, c0, p['gdn_conv_w'], p['gdn_dt_bias'], p['gdn_dt_bias'].reshape(-1, 1),
            p['gdn_a_log'], p['gdn_a_log'].reshape(-1, 1), p['gdn_norm']]
    aliases = {}
    if o_prev is not None:
        in_specs.append(pl.BlockSpec(memory_space=pl.ANY))
        args.append(o_prev)
        aliases = {len(args) - 1: 0}

    def body(*refs):
        if o_prev is not None:
            refs = refs[:11] + refs[12:]
        _gdn_kernel(*refs, q=q, nc=nc)

    return pl.pallas_call(
        body,
        grid=(bsz, nc),
        in_specs=in_specs,
        out_specs=[pl.BlockSpec((q, GDN_VD), tok), st_spec, cv_spec],
        out_shape=[jax.ShapeDtypeStruct((t, GDN_VD), BF16),
                   jax.ShapeDtypeStruct((bsz, GDN_V_HEADS, GDN_K_DIM, GDN_V_DIM), F32),
                   jax.ShapeDtypeStruct((bsz, CONV_TAIL, GDN_CONV_DIM), F32)],
        scratch_shapes=[pltpu.VMEM((q + SUBLANE, GDN_CONV_DIM), F32),
                        pltpu.VMEM((GDN_V_HEADS, GDN_K_DIM, GDN_V_DIM), F32)],
        input_output_aliases=aliases,
        compiler_params=_cparams("parallel", "arbitrary"),
        name="gdn_scan",
    )(*args)


def _hgrn_kernel(q_ref, f_ref, i_ref, gz_ref, s0_ref, lb_ref, nw_ref,
                 o_ref, sn_ref, st_ref, *, q, nc):
    c = pl.program_id(1)
    nsub = q // HGRN_SUB

    @pl.when(c == 0)
    def _():
        for h in range(HGRN_HEADS):
            st_ref[h] = s0_ref[0, h].T

    lb = lb_ref[...]
    f = lb + (1.0 - lb) * jax.nn.sigmoid(f_ref[...])
    logf = jnp.log(f)
    kk = 1.0 - f
    qs = _silu(q_ref[...].astype(F32))
    row = _iota2((q, q), 0)
    col = _iota2((q, q), 1)
    same = (row // HGRN_SUB) == (col // HGRN_SUB)
    intra = same & (row >= col)
    m_pref = jnp.where(intra, 1.0, 0.0).astype(BF16)
    m_blk = jnp.where(same, 1.0, 0.0).astype(BF16)
    lh, ll = _split(logf)
    gam = _dot(m_pref, lh) + _dot(m_pref, ll)
    gtot = _dot(m_blk, lh) + _dot(m_blk, ll)
    qt = qs * jnp.exp(gam)
    kt = kk * jnp.exp(jnp.minimum(-gam, EXP_CLAMP))
    kd = kk * jnp.exp(gtot - gam)
    edec = jnp.exp(gtot)
    vv = i_ref[...].astype(F32)
    gz = gz_ref[...].astype(F32)
    rsub = _iota2((q, 1), 0) // HGRN_SUB

    heads = range(HGRN_HEADS)
    sls = [slice(h * HGRN_K_DIM, (h + 1) * HGRN_K_DIM) for h in heads]
    qt_b = [qt[:, sl].astype(BF16) for sl in sls]
    v_t = [vv[:, sl].T.astype(BF16) for sl in sls]
    o_in = []
    for h in heads:
        att = jnp.where(intra, _dot_nt(qt_b[h], kt[:, sls[h]].astype(BF16)), 0.0)
        o_in.append(_dot(att.astype(BF16), vv[:, sls[h]].astype(BF16)))

    def window_update(j):
        return [_dot(v_t[h], jnp.where(rsub == j, kd[:, sls[h]], 0.0).astype(BF16)) for h in heads]

    s_t = [st_ref[h] for h in heads]
    o_x = [[] for _ in heads]
    upd_next = window_update(0)
    for j in range(nsub):
        upd = upd_next
        if j + 1 < nsub:
            upd_next = window_update(j + 1)
        r0 = j * HGRN_SUB
        for h in heads:
            o_x[h].append(_dot_nt(qt_b[h][r0:r0 + HGRN_SUB, :], s_t[h].astype(BF16)))
            s_t[h] = s_t[h] * edec[r0:r0 + 1, sls[h]] + upd[h]
    outs = []
    for h in heads:
        st_ref[h] = s_t[h]
        o = o_in[h] + jnp.concatenate(o_x[h], axis=0)
        ms = jnp.mean(o * o, axis=-1, keepdims=True)
        outs.append(o * lax.rsqrt(ms + EPS) * nw_ref[...] * _silu(gz[:, sls[h]]))
    o_ref[...] = jnp.concatenate(outs, axis=1).astype(o_ref.dtype)

    @pl.when(c == nc - 1)
    def _():
        for h in range(HGRN_HEADS):
            sn_ref[0, h] = st_ref[h].T


def hgrn_scan(qp, fz, iv, gz, s0, lb, nw, row0, bsz, seqlen, q, o_prev=None):
    nc = seqlen // q
    blk0 = row0 // q
    assert row0 % q == 0 and seqlen % q == 0 and q % HGRN_SUB == 0
    hk = HGRN_HEADS * HGRN_K_DIM
    hv = HGRN_HEADS * HGRN_V_DIM
    tok = lambda b, c: (blk0 + b * nc + c, 0)
    fixed2 = lambda b, c: (0, 0)
    t = qp.shape[0]
    st_spec = pl.BlockSpec((1, HGRN_HEADS, HGRN_K_DIM, HGRN_V_DIM), lambda b, c: (b, 0, 0, 0))
    in_specs = [pl.BlockSpec((q, hk), tok), pl.BlockSpec((q, hk), tok),
                pl.BlockSpec((q, hv), tok), pl.BlockSpec((q, hv), tok),
                st_spec, pl.BlockSpec((1, hk), fixed2), pl.BlockSpec((1, HGRN_V_DIM), fixed2)]
    args = [qp, fz, iv, gz, s0, lb, nw]
    aliases = {}
    if o_prev is not None:
        in_specs.append(pl.BlockSpec(memory_space=pl.ANY))
        args.append(o_prev)
        aliases = {len(args) - 1: 0}

    def body(*refs):
        if o_prev is not None:
            refs = refs[:7] + refs[8:]
        _hgrn_kernel(*refs, q=q, nc=nc)

    return pl.pallas_call(
        body,
        grid=(bsz, nc),
        in_specs=in_specs,
        out_specs=[pl.BlockSpec((q, hv), tok), st_spec],
        out_shape=[jax.ShapeDtypeStruct((t, hv), BF16),
                   jax.ShapeDtypeStruct((bsz, HGRN_HEADS, HGRN_K_DIM, HGRN_V_DIM), F32)],
        scratch_shapes=[pltpu.VMEM((HGRN_HEADS, HGRN_V_DIM, HGRN_K_DIM), F32)],
        input_output_aliases=aliases,
        compiler_params=_cparams("parallel", "arbitrary"),
        name="hgrn_scan",
    )(*args)


def _out_proj_kernel(*refs, n_in):
    a_refs = refs[:n_in]
    w_refs = refs[n_in:2 * n_in]
    h_ref, o_ref = refs[2 * n_in], refs[2 * n_in + 1]
    acc = h_ref[...]
    for a_ref, w_ref in zip(a_refs, w_refs):
        acc = acc + _dot(a_ref[...], w_ref[...])
    o_ref[...] = acc


def out_proj(acts, ws, h, tm):
    t, d = h.shape
    n_in = len(acts)
    return pl.pallas_call(
        functools.partial(_out_proj_kernel, n_in=n_in),
        grid=(t // tm,),
        in_specs=([pl.BlockSpec((tm, a.shape[1]), lambda i: (i, 0)) for a in acts]
                  + [pl.BlockSpec(w.shape, lambda i: (0, 0)) for w in ws]
                  + [pl.BlockSpec((tm, d), lambda i: (i, 0))]),
        out_specs=pl.BlockSpec((tm, d), lambda i: (i, 0)),
        out_shape=jax.ShapeDtypeStruct((t, d), F32),
        compiler_params=_cparams("parallel"),
        name="out_proj",
    )(*acts, *ws, h)


R_E0, R_E1, R_RANK0, R_RANK1, R_GATE0, R_GATE1 = range(6)
ROUTE_ROWS = 8
ROUTER_ROWS = SUBLANE + MOE_EXPERTS


def _router_kernel(h_ref, g_ref, w_ref, xn_ref, info_ref, cnt_ref, carry_ref, *, tm, nt):
    i = pl.program_id(0)

    @pl.when(i == 0)
    def _():
        carry_ref[...] = jnp.zeros_like(carry_ref)

    x = h_ref[...]
    ms = jnp.mean(x * x, axis=-1, keepdims=True)
    xn = x * lax.rsqrt(ms + EPS) * g_ref[...]
    xn_ref[...] = xn
    logit = _dot3_nt(w_ref[...], xn)
    lg = logit[:MOE_GROUPS, :]
    gmax = jnp.max(lg, axis=0, keepdims=True)
    gidx = _iota2(lg.shape, 0)
    g_sel = jnp.min(jnp.where(lg == gmax, gidx, MOE_GROUPS), axis=0, keepdims=True)
    g_prob = 1.0 / jnp.sum(jnp.exp(lg - gmax), axis=0, keepdims=True)
    el = jnp.zeros((MOE_EPG, tm), F32)
    for g in range(MOE_GROUPS):
        el = jnp.where(g_sel == g, logit[SUBLANE + g * MOE_EPG:SUBLANE + (g + 1) * MOE_EPG, :], el)
    eidx = _iota2(el.shape, 0)
    m1 = jnp.max(el, axis=0, keepdims=True)
    i1 = jnp.min(jnp.where(el == m1, eidx, MOE_EPG), axis=0, keepdims=True)
    el2 = jnp.where(eidx == i1, -jnp.inf, el)
    m2 = jnp.max(el2, axis=0, keepdims=True)
    i2 = jnp.min(jnp.where(el2 == m2, eidx, MOE_EPG), axis=0, keepdims=True)
    ex = jnp.exp(m2 - m1)
    gate0 = g_prob / (1.0 + ex)
    gate1 = g_prob * ex / (1.0 + ex)
    e0 = g_sel * MOE_EPG + i1
    e1 = g_sel * MOE_EPG + i2

    xid = _iota2((MOE_EXPERTS, tm), 0)
    oh0 = xid == e0
    oh1 = xid == e1
    onehot = jnp.where(oh0 | oh1, 1.0, 0.0).astype(BF16)
    upp = jnp.where(_iota2((tm, tm), 0) <= _iota2((tm, tm), 1), 1.0, 0.0).astype(BF16)
    pref = _dot(onehot, upp)
    base = carry_ref[:, 0:1] + pref - 1.0
    rank0 = jnp.sum(jnp.where(oh0, base, 0.0), axis=0, keepdims=True)
    rank1 = jnp.sum(jnp.where(oh1, base, 0.0), axis=0, keepdims=True)
    new_carry = carry_ref[...] + pref[:, tm - 1:tm]
    carry_ref[...] = new_carry

    info_ref[R_E0:R_E0 + 1, :] = e0.astype(F32)
    info_ref[R_E1:R_E1 + 1, :] = e1.astype(F32)
    info_ref[R_RANK0:R_RANK0 + 1, :] = rank0
    info_ref[R_RANK1:R_RANK1 + 1, :] = rank1
    info_ref[R_GATE0:R_GATE0 + 1, :] = gate0
    info_ref[R_GATE1:R_GATE1 + 1, :] = gate1
    info_ref[R_GATE1 + 1:ROUTE_ROWS, :] = jnp.zeros((ROUTE_ROWS - R_GATE1 - 1, tm), F32)

    @pl.when(i == nt - 1)
    def _():
        cnt_ref[...] = new_carry


def moe_router(h, gain, w_rt, tm):
    t, d = h.shape
    nt = t // tm
    return pl.pallas_call(
        functools.partial(_router_kernel, tm=tm, nt=nt),
        grid=(nt,),
        in_specs=[pl.BlockSpec((tm, d), lambda i: (i, 0)),
                  pl.BlockSpec((1, d), lambda i: (0, 0)),
                  pl.BlockSpec((ROUTER_ROWS, d), lambda i: (0, 0))],
        out_specs=[pl.BlockSpec((tm, d), lambda i: (i, 0)),
                   pl.BlockSpec((ROUTE_ROWS, tm), lambda i: (0, i)),
                   pl.BlockSpec((MOE_EXPERTS, LANE), lambda i: (0, 0))],
        out_shape=[jax.ShapeDtypeStruct((t, d), F32),
                   jax.ShapeDtypeStruct((ROUTE_ROWS, t), F32),
                   jax.ShapeDtypeStruct((MOE_EXPERTS, LANE), F32)],
        scratch_shapes=[pltpu.VMEM((MOE_EXPERTS, LANE), F32)],
        compiler_params=_cparams("arbitrary"),
        name="moe_router",
    )(h, gain.reshape(1, d), w_rt)


def _row_dma_loops(row_copy, tm):
    def each(fn):
        def body(g, carry):
            base = pl.multiple_of(g * SUBLANE, SUBLANE)
            for r in range(SUBLANE):
                fn(base + r)
            return carry
        return lambda: lax.fori_loop(0, tm // SUBLANE, body, 0)

    def start(row):
        row_copy(row, 0).start(priority=0)
        row_copy(row, 1).start(priority=1)

    def wait(row):
        row_copy(row, 0).wait()
        row_copy(row, 1).wait()

    return each(start), each(wait)


def _dispatch_kernel(pad_end_ref, own_end_ref, dest_ref, xn_ref, *rest, tm, first):
    buf_ref, zero_ref, sem, zsem = rest[-4:]

    def zero_padding_tiles():
        zero_ref[...] = jnp.zeros_like(zero_ref)

        def tile_copy(j):
            return pltpu.make_async_copy(
                zero_ref, buf_ref.at[pl.ds(pl.multiple_of(j * MOE_ROW_TILE, MOE_ROW_TILE), MOE_ROW_TILE)], zsem)

        spans = [(own_end_ref[e] // MOE_ROW_TILE, pad_end_ref[e] // MOE_ROW_TILE) for e in range(MOE_EXPERTS)]
        spans.append((pad_end_ref[MOE_EXPERTS - 1] // MOE_ROW_TILE, buf_ref.shape[0] // MOE_ROW_TILE))
        for lo, hi in spans:
            lax.fori_loop(lo, hi, lambda j, c: (tile_copy(j).start(), c)[1], 0)
        for lo, hi in spans:
            lax.fori_loop(lo, hi, lambda j, c: (tile_copy(j).wait(), c)[1], 0)

    if first:
        pl.when(pl.program_id(0) == 0)(zero_padding_tiles)

    def row_copy(row, k):
        return pltpu.make_async_copy(xn_ref.at[pl.ds(row, 1)],
                                     buf_ref.at[pl.ds(dest_ref[0, 0, k * tm + row], 1)], sem)

    start_all, wait_all = _row_dma_loops(row_copy, tm)
    start_all()
    wait_all()


def moe_dispatch(xn, dest3, pad_end, own_end, n_rows, tm, buf_prev=None):
    t, d = xn.shape
    in_specs = [pl.BlockSpec((1, 1, 2 * tm), lambda i, pe, pd: (i, 0, 0), memory_space=pltpu.SMEM),
                pl.BlockSpec((tm, d), lambda i, pe, pd: (i, 0))]
    args = [pad_end, own_end, dest3, xn]
    aliases = {}
    if buf_prev is not None:
        in_specs.append(pl.BlockSpec(memory_space=pl.ANY))
        args.append(buf_prev)
        aliases = {len(args) - 1: 0}
    return pl.pallas_call(
        functools.partial(_dispatch_kernel, tm=tm, first=buf_prev is None),
        grid_spec=pltpu.PrefetchScalarGridSpec(
            num_scalar_prefetch=2,
            grid=(t // tm,),
            in_specs=in_specs,
            out_specs=pl.BlockSpec(memory_space=pl.ANY),
            scratch_shapes=[pltpu.VMEM((MOE_ROW_TILE, d), xn.dtype),
                            pltpu.SemaphoreType.DMA(()), pltpu.SemaphoreType.DMA(())],
        ),
        out_shape=jax.ShapeDtypeStruct((n_rows, d), xn.dtype),
        input_output_aliases=aliases,
        compiler_params=_cparams("arbitrary"),
        name="moe_dispatch",
    )(*args)


def _expert_kernel(te_ref, nu_ref, x_ref, wg_ref, wu_ref, wd_ref, y_ref, wg_b, wu_b, wd_b):
    i = pl.program_id(0)

    @pl.when((i == 0) | (te_ref[i] != te_ref[jnp.maximum(i - 1, 0)]))
    def _():
        wg_b[...] = wg_ref[0].astype(BF16)
        wu_b[...] = wu_ref[0].astype(BF16)
        wd_b[...] = wd_ref[0].astype(BF16)

    @pl.when(i < nu_ref[0])
    def _():
        x = x_ref[...].astype(BF16)
        hid = _silu(_dot(x, wg_b[...])) * _dot(x, wu_b[...])
        y_ref[...] = _dot(hid.astype(BF16), wd_b[...])

    @pl.when(i >= nu_ref[0])
    def _():
        y_ref[...] = jnp.zeros_like(y_ref)


def moe_experts(buf, tile_e, n_used, wg, wu, wd, first_expert):
    n_rows, d = buf.shape
    n_tiles = n_rows // MOE_ROW_TILE
    return pl.pallas_call(
        _expert_kernel,
        grid_spec=pltpu.PrefetchScalarGridSpec(
            num_scalar_prefetch=2,
            grid=(n_tiles,),
            in_specs=[pl.BlockSpec((MOE_ROW_TILE, d), lambda i, te, nu: (jnp.minimum(i, nu[0] - 1), 0)),
                      pl.BlockSpec((1, d, MOE_D_EXPERT), lambda i, te, nu: (first_expert + te[i], 0, 0)),
                      pl.BlockSpec((1, d, MOE_D_EXPERT), lambda i, te, nu: (first_expert + te[i], 0, 0)),
                      pl.BlockSpec((1, MOE_D_EXPERT, d), lambda i, te, nu: (first_expert + te[i], 0, 0))],
            out_specs=pl.BlockSpec((MOE_ROW_TILE, d), lambda i, te, nu: (i, 0)),
            scratch_shapes=[pltpu.VMEM((d, MOE_D_EXPERT), BF16), pltpu.VMEM((d, MOE_D_EXPERT), BF16),
                            pltpu.VMEM((MOE_D_EXPERT, d), BF16)],
        ),
        out_shape=jax.ShapeDtypeStruct((n_rows, d), F32),
        compiler_params=_cparams("arbitrary"),
        name="moe_experts",
    )(tile_e, n_used, buf, wg, wu, wd)


def _combine_kernel(dest_ref, h_ref, info_ref, fg_ref, yb_ref, o_ref, ybuf, sem, *, tm, final_norm):
    def row_copy(row, k):
        return pltpu.make_async_copy(yb_ref.at[pl.ds(dest_ref[0, 0, k * tm + row], 1)],
                                     ybuf.at[k, pl.ds(row, 1)], sem)

    start_all, wait_all = _row_dma_loops(row_copy, tm)
    start_all()
    info_t = info_ref[...].T
    g0 = info_t[:, R_GATE0:R_GATE0 + 1]
    g1 = info_t[:, R_GATE1:R_GATE1 + 1]
    wait_all()
    y = h_ref[...] + (g0 * ybuf[0] + g1 * ybuf[1])
    if final_norm:
        ms = jnp.mean(y * y, axis=-1, keepdims=True)
        y = y * lax.rsqrt(ms + EPS) * fg_ref[...]
    o_ref[...] = y


def moe_combine(h, info, dest3, yb, final_gain, final_norm, tm):
    t, d = h.shape
    return pl.pallas_call(
        functools.partial(_combine_kernel, tm=tm, final_norm=final_norm),
        grid=(t // tm,),
        in_specs=[pl.BlockSpec((1, 1, 2 * tm), lambda i: (i, 0, 0), memory_space=pltpu.SMEM),
                  pl.BlockSpec((tm, d), lambda i: (i, 0)),
                  pl.BlockSpec((ROUTE_ROWS, tm), lambda i: (0, i)),
                  pl.BlockSpec((1, d), lambda i: (0, 0)),
                  pl.BlockSpec(memory_space=pl.ANY)],
        out_specs=pl.BlockSpec((tm, d), lambda i: (i, 0)),
        out_shape=jax.ShapeDtypeStruct((t, d), F32),
        scratch_shapes=[pltpu.VMEM((2, tm, d), F32), pltpu.SemaphoreType.DMA(())],
        compiler_params=_cparams("arbitrary"),
        name="moe_combine",
    )(dest3, h, info, final_gain.reshape(1, d), yb)


def hier_moe_residual(hs, tms, gain, w_rt, wg, wu, wd, first_expert, final_gain, final_norm):
    d = hs[0].shape[1]
    routed = [moe_router(h, gain, w_rt, tm) for h, tm in zip(hs, tms)]
    counts = [cnt[:, 0].astype(jnp.int32) for _, _, cnt in routed]
    total = sum(counts)
    padded = (total + MOE_ROW_TILE - 1) // MOE_ROW_TILE * MOE_ROW_TILE
    pad_end = jnp.cumsum(padded).astype(jnp.int32)
    pad_start = pad_end - padded
    n_tiles = 2 * sum(h.shape[0] for h in hs) // MOE_ROW_TILE + MOE_EXPERTS
    n_used = (pad_end[-1] // MOE_ROW_TILE).astype(jnp.int32)
    tile_row = jnp.minimum(jnp.arange(n_tiles, dtype=jnp.int32), n_used - 1) * MOE_ROW_TILE
    tile_e = jnp.sum(tile_row[:, None] >= pad_end[None, :], axis=1).astype(jnp.int32)
    buf = None
    dests = []
    first_row = pad_start
    for (xn, info, _), cnt, tm in zip(routed, counts, tms):
        t = xn.shape[0]
        e01 = info[R_E0:R_E1 + 1].astype(jnp.int32)
        rank01 = info[R_RANK0:R_RANK1 + 1].astype(jnp.int32)
        onehot = e01[:, :, None] == jnp.arange(MOE_EXPERTS, dtype=jnp.int32)
        dest = rank01 + jnp.sum(jnp.where(onehot, first_row, 0), axis=-1)
        dest3 = dest.reshape(2, t // tm, tm).transpose(1, 0, 2).reshape(t // tm, 1, 2 * tm)
        buf = moe_dispatch(xn, dest3, pad_end, first_row + cnt, n_tiles * MOE_ROW_TILE, tm, buf)
        dests.append(dest3)
        first_row = first_row + cnt
    yb = moe_experts(buf, tile_e, n_used.reshape(1), wg, wu, wd, first_expert)
    return [moe_combine(h, info, dest3, yb, final_gain, final_norm, tm)
            for h, (_, info, _), dest3, tm in zip(hs, routed, dests, tms)]


def _cast_split_kernel(w_ref, *o_refs, pieces):
    w = w_ref[...]
    for o_ref, cols in zip(o_refs, pieces):
        parts = [w[:, a:b] for a, b in cols]
        width = sum(b - a for a, b in cols)
        if width < o_ref.shape[1]:
            parts.append(jnp.zeros((w.shape[0], o_ref.shape[1] - width), F32))
        o_ref[...] = (parts[0] if len(parts) == 1 else jnp.concatenate(parts, axis=1)).astype(BF16)


def cast_split(w, pieces, widths, tk, name):
    k, n = w.shape
    assert k % tk == 0
    return pl.pallas_call(
        functools.partial(_cast_split_kernel, pieces=pieces),
        grid=(k // tk,),
        in_specs=[pl.BlockSpec((tk, n), lambda i: (i, 0))],
        out_specs=[pl.BlockSpec((tk, wd), lambda i: (i, 0)) for wd in widths],
        out_shape=[jax.ShapeDtypeStruct((k, wd), BF16) for wd in widths],
        compiler_params=_cparams("parallel"),
        name=name,
    )(w)


def _prep_ab_weights(w_in_ab):
    o = [int(v) for v in np.cumsum([0, SSD_INNER, SSD_CONV_DIM, SSD_HEADS, GDN_CONV_DIM, GDN_VD,
                                    GDN_V_HEADS, GDN_V_HEADS])]
    seg = lambda k: (o[k], o[k + 1])
    pieces = ((seg(0),), (seg(1),), (seg(3),), (seg(4),), (seg(2), seg(5), seg(6)))
    widths = (SSD_INNER, SSD_CONV_DIM, GDN_CONV_DIM, GDN_VD, SMALL_W)
    return cast_split(w_in_ab, pieces, widths, 128, "prep_w_in_ab")


def _prep_c_weights(w_in_c):
    pieces = tuple(((k * D_MODEL, (k + 1) * D_MODEL),) for k in range(4))
    return cast_split(w_in_c, pieces, (D_MODEL,) * 4, 128, "prep_w_in_c")


def _ab_layer(h, states, w, g):
    st_ssd, st_ssd_conv, st_gdn, st_gdn_conv, _ = states
    zs, xbc, qkv, zg, small = norm_proj(h, w['norm_mix'][0], w['w_in_ab'], (BF16, BF16, BF16, BF16, F32),
                                        g['tm'], "in_proj_ab")
    y, n_ssd, n_ssd_conv = ssd_scan(xbc, small, zs, st_ssd[:, 0], st_ssd_conv[:, 0], w, 0,
                                    g['bsz'], g['seqlen'], g['q_ssd'])
    o, n_gdn, n_gdn_conv = gdn_scan(qkv, small, zg, st_gdn[:, 0], st_gdn_conv[:, 0], w, 0,
                                    g['bsz'], g['seqlen'], g['q_gdn'])
    h = out_proj([y, o], [w['w_out_ab'][:SSD_INNER], w['w_out_ab'][SSD_INNER:]], h, g['tm'])
    return h, (n_ssd[:, None], n_ssd_conv[:, None], n_gdn[:, None], n_gdn_conv[:, None])


def _c_layer(h, states, w, g):
    qp, fz, iv, gz = norm_proj(h, w['norm_mix'][1], w['w_in_c'], (BF16, F32, BF16, BF16), g['tm'], "in_proj_c")
    oc, n_hgrn = hgrn_scan(qp, fz, iv, gz, states[4][:, 0], w['hgrn_lb'], w['hgrn_norm'], 0,
                           g['bsz'], g['seqlen'], g['q_hgrn'])
    return out_proj([oc], [w['w_out_c']], h, g['tm']), n_hgrn[:, None]


def _moe_layer(hs, groups, w, li, final_norm):
    return hier_moe_residual(hs, [g['tm'] for g in groups], w['norm_ffn'][li], w['moe_w_rt'][li],
                             w['moe_w_gate'], w['moe_w_up'], w['moe_w_down'], li * MOE_EXPERTS,
                             w['norm_final'], final_norm)


def kernel(x_prompt, x_sample, state_ssd, state_ssd_conv, state_gdn, state_gdn_conv, state_hgrn,
           norm_mix, norm_ffn, norm_final,
           w_in_ab, ssd_conv_w, ssd_conv_b, ssd_dt_bias, ssd_a_log, ssd_d, ssd_norm,
           gdn_conv_w, gdn_dt_bias, gdn_a_log, gdn_norm, w_out_ab,
           w_in_c, hgrn_lb_logits, hgrn_norm, w_out_c,
           moe_w_group, moe_w_router, moe_w_gate, moe_w_up, moe_w_down):
    depth = norm_mix.shape[0]
    sm = jax.nn.softmax(hgrn_lb_logits.astype(F32), axis=0)
    lower_bounds = jnp.cumsum(sm, axis=0) - sm[0]
    w_rt = jnp.concatenate([jnp.swapaxes(moe_w_group, 1, 2),
                            jnp.zeros((depth, SUBLANE - MOE_GROUPS, D_MODEL), F32),
                            jnp.swapaxes(moe_w_router, 1, 2)], axis=1)
    expand = (jnp.arange(SSD_INNER)[None, :] // SSD_HEAD_DIM == jnp.arange(SSD_HEADS)[:, None]).astype(F32)
    w = dict(
        norm_mix=norm_mix, norm_ffn=norm_ffn, norm_final=norm_final,
        w_in_ab=_prep_ab_weights(w_in_ab[0]),
        ssd_conv_w=ssd_conv_w[0], ssd_conv_b=ssd_conv_b[0][None, :], ssd_dt_bias=ssd_dt_bias[0][None, :],
        ssd_a_log=ssd_a_log[0][None, :], ssd_d_full=jnp.repeat(ssd_d[0], SSD_HEAD_DIM)[None, :],
        ssd_norm=ssd_norm[0][None, :], ssd_expand=expand,
        gdn_conv_w=gdn_conv_w[0], gdn_dt_bias=gdn_dt_bias[0][None, :], gdn_a_log=gdn_a_log[0][None, :],
        gdn_norm=gdn_norm[0][None, :], w_out_ab=w_out_ab[0].astype(BF16),
        w_in_c=_prep_c_weights(w_in_c[0]),
        hgrn_lb=lower_bounds[1][None, :], hgrn_norm=hgrn_norm[0][None, :],
        w_out_c=w_out_c[0].astype(BF16), moe_w_rt=w_rt,
        moe_w_gate=moe_w_gate.reshape(-1, D_MODEL, MOE_D_EXPERT),
        moe_w_up=moe_w_up.reshape(-1, D_MODEL, MOE_D_EXPERT),
        moe_w_down=moe_w_down.reshape(-1, MOE_D_EXPERT, D_MODEL),
    )
    bp, lp, _ = x_prompt.shape
    bs, ls, _ = x_sample.shape
    zeros = lambda ref: jnp.zeros((bp,) + ref.shape[1:], x_prompt.dtype)
    p_states = tuple(zeros(s) for s in (state_ssd, state_ssd_conv, state_gdn, state_gdn_conv, state_hgrn))
    s_states = (state_ssd, state_ssd_conv, state_gdn, state_gdn_conv, state_hgrn)
    groups = [dict(bsz=bp, seqlen=lp, q_ssd=256, q_gdn=128, q_hgrn=256, tm=512),
              dict(bsz=bs, seqlen=ls, q_ssd=ls, q_gdn=ls, q_hgrn=ls, tm=bs * ls)]
    states = [p_states, s_states]
    hs = [x_prompt.reshape(bp * lp, D_MODEL), x_sample.reshape(bs * ls, D_MODEL)]
    ab = [_ab_layer(h, st, w, g) for h, st, g in zip(hs, states, groups)]
    hs = _moe_layer([a[0] for a in ab], groups, w, 0, False)
    cl = [_c_layer(h, st, w, g) for h, st, g in zip(hs, states, groups)]
    ys = _moe_layer([c[0] for c in cl], groups, w, 1, True)
    y_p, y_s = (y.reshape(g['bsz'], g['seqlen'], D_MODEL) for y, g in zip(ys, groups))
    return (y_p, y_s) + ab[0][1] + (cl[0][1],) + ab[1][1] + (cl[1][1],)
```

```python
import functools

import jax
import jax.numpy as jnp
import numpy as np
from jax import lax
from jax.experimental import pallas as pl
from jax.experimental.pallas import tpu as pltpu

F32 = jnp.float32
BF16 = jnp.bfloat16

D_MODEL = 1024
EPS = 1e-6
CONV_W = 4
CONV_TAIL = CONV_W - 1
SSD_HEADS = 16
SSD_HEAD_DIM = 64
SSD_INNER = SSD_HEADS * SSD_HEAD_DIM
SSD_GROUPS = 2
SSD_STATE = 128
SSD_CONV_DIM = SSD_INNER + 2 * SSD_GROUPS * SSD_STATE
GDN_K_HEADS = 4
GDN_V_HEADS = 8
GDN_K_DIM = 128
GDN_V_DIM = 128
GDN_QK = GDN_K_HEADS * GDN_K_DIM
GDN_VD = GDN_V_HEADS * GDN_V_DIM
GDN_CONV_DIM = 2 * GDN_QK + GDN_VD
HGRN_HEADS = 8
HGRN_K_DIM = 128
HGRN_V_DIM = 128
HGRN_SUB = 32
MOE_GROUPS = 4
MOE_EPG = 8
MOE_EXPERTS = MOE_GROUPS * MOE_EPG
MOE_D_EXPERT = 512
MOE_ROW_TILE = 512
SMALL_W = 128
EXP_CLAMP = 80.0

LANE = 128
SUBLANE = 8
VMEM_LIMIT = 56 * 1024 * 1024


def _cparams(*sem):
    return pltpu.CompilerParams(dimension_semantics=sem, vmem_limit_bytes=VMEM_LIMIT)


def _dot(a, b):
    return jnp.dot(a, b, preferred_element_type=F32)


def _dot_nt(a, b):
    return lax.dot_general(a, b, (((1,), (1,)), ((), ())), preferred_element_type=F32)


def _split(x):
    hi = x.astype(BF16)
    lo = (x - hi.astype(F32)).astype(BF16)
    return hi, lo


def _dot3(a, b):
    ah, al = _split(a)
    bh, bl = _split(b)
    return _dot(ah, bh) + (_dot(ah, bl) + _dot(al, bh))


def _dot3_nt(a, b):
    ah, al = _split(a)
    bh, bl = _split(b)
    return _dot_nt(ah, bh) + (_dot_nt(ah, bl) + _dot_nt(al, bh))


def _dot2_exact_lhs(m_bf16, x):
    xh, xl = _split(x)
    return _dot(m_bf16, xh) + _dot(m_bf16, xl)


def _silu(x):
    return x * jax.nn.sigmoid(x)


def _softplus(x):
    return jnp.maximum(x, 0.0) + jnp.log1p(jnp.exp(-jnp.abs(x)))


def _iota2(shape, dim):
    return lax.broadcasted_iota(jnp.int32, shape, dim)


def _norm_proj_kernel(*refs, n_out):
    x_ref, g_ref = refs[:2]
    w_refs = refs[2:2 + n_out]
    o_refs = refs[2 + n_out:]
    x = x_ref[...]
    ms = jnp.mean(x * x, axis=-1, keepdims=True)
    u = (x * lax.rsqrt(ms + EPS) * g_ref[...]).astype(BF16)
    for w_ref, o_ref in zip(w_refs, o_refs):
        o_ref[...] = _dot(u, w_ref[...]).astype(o_ref.dtype)


def norm_proj(x, gain, ws, dtypes, tm, name):
    t, d = x.shape
    assert t % tm == 0
    return pl.pallas_call(
        functools.partial(_norm_proj_kernel, n_out=len(ws)),
        grid=(t // tm,),
        in_specs=([pl.BlockSpec((tm, d), lambda i: (i, 0)), pl.BlockSpec((1, d), lambda i: (0, 0))]
                  + [pl.BlockSpec(w.shape, lambda i: (0, 0)) for w in ws]),
        out_specs=[pl.BlockSpec((tm, w.shape[1]), lambda i: (i, 0)) for w in ws],
        out_shape=[jax.ShapeDtypeStruct((t, w.shape[1]), dt) for w, dt in zip(ws, dtypes)],
        compiler_params=_cparams("parallel"),
        name=name,
    )(x, gain.reshape(1, d), *ws)


def _conv_chunk(x_ref, c0_ref, cn_ref, ext_ref, w_ref, first, last, q):
    @pl.when(first)
    def _():
        ext_ref[SUBLANE - CONV_TAIL:SUBLANE, :] = c0_ref[0]

    ext_ref[SUBLANE:SUBLANE + q, :] = x_ref[...].astype(F32)
    acc = ext_ref[SUBLANE:SUBLANE + q, :] * w_ref[CONV_TAIL:CONV_W, :]
    for j in range(CONV_TAIL):
        lo = SUBLANE - CONV_TAIL + j
        acc = acc + ext_ref[lo:lo + q, :] * w_ref[j:j + 1, :]
    tail = ext_ref[SUBLANE + q - CONV_TAIL:SUBLANE + q, :]

    @pl.when(last)
    def _():
        cn_ref[0] = tail

    ext_ref[SUBLANE - CONV_TAIL:SUBLANE, :] = tail
    return acc


def _ssd_kernel(xbc_ref, sm_ref, z_ref, s0_ref, c0_ref, cw_ref, cb_ref, dtb_ref, dtbt_ref,
                alog_ref, alogt_ref, dfull_ref, nw_ref, exp_ref,
                y_ref, sn_ref, cn_ref, ext_ref, st_ref, *, q, nc):
    c = pl.program_id(1)
    first = c == 0
    last = c == nc - 1
    hp = SSD_HEADS // SSD_GROUPS * SSD_HEAD_DIM
    npair = SSD_HEADS // 2

    @pl.when(first)
    def _():
        for pr in range(npair):
            g, j = divmod(pr, npair // SSD_GROUPS)
            st_ref[g, :, j * LANE:(j + 1) * LANE] = s0_ref[0, pr].T

    xbc = _conv_chunk(xbc_ref, c0_ref, cn_ref, ext_ref, cw_ref, first, last, q) + cb_ref[...]
    xbc = _silu(xbc)
    xs = xbc[:, :SSD_INNER]
    xs_b = xs.astype(BF16)

    sm = sm_ref[...]
    sm_t = sm.T
    dt = _softplus(sm[:, :SSD_HEADS] + dtb_ref[...])
    dt_t = _softplus(sm_t[:SSD_HEADS, :] + dtbt_ref[...])
    a = -jnp.exp(alog_ref[...]) * dt
    a_t = -jnp.exp(alogt_ref[...]) * dt_t
    row = _iota2((q, q), 0)
    col = _iota2((q, q), 1)
    causal = row >= col
    low = jnp.where(causal, 1.0, 0.0).astype(BF16)
    upp = jnp.where(row <= col, 1.0, 0.0).astype(BF16)
    cum = _dot2_exact_lhs(low, a)
    ah, al = _split(a_t)
    cum_t = _dot(ah, upp) + _dot(al, upp)
    cum_last = cum[q - 1:q, :]
    ecum = jnp.exp(cum)
    tail = jnp.exp(cum_last - cum) * dt
    ecum_f = _dot3(ecum, exp_ref[...])
    tail_f = _dot3(tail, exp_ref[...])
    dlast_f = _dot3(jnp.exp(cum_last), exp_ref[...])

    lane = _iota2((q, LANE), 1)
    y_parts = []
    for g in range(SSD_GROUPS):
        bm = xbc[:, SSD_INNER + g * SSD_STATE:SSD_INNER + (g + 1) * SSD_STATE]
        cm = xbc[:, SSD_INNER + (SSD_GROUPS + g) * SSD_STATE:SSD_INNER + (SSD_GROUPS + g + 1) * SSD_STATE]
        bm_b = bm.astype(BF16)
        cm_b = cm.astype(BF16)
        gmat = _dot_nt(cm_b, bm_b)
        st = st_ref[g]
        y_int = _dot(cm_b, st.astype(BF16)) * ecum_f[:, g * hp:(g + 1) * hp]
        for j in range(npair // SSD_GROUPS):
            pr = g * (npair // SSD_GROUPS) + j
            xp = xs_b[:, pr * LANE:(pr + 1) * LANE]
            ys = []
            for hh in range(2):
                h = 2 * pr + hh
                diff = cum[:, h:h + 1] - cum_t[h:h + 1, :]
                dec = jnp.exp(jnp.where(causal, diff, -jnp.inf)) * dt_t[h:h + 1, :]
                ys.append(_dot((gmat * dec).astype(BF16), xp))
            y_parts.append(jnp.where(lane < SSD_HEAD_DIM, ys[0], ys[1]) + y_int[:, j * LANE:(j + 1) * LANE])
        xsc = (xs[:, g * hp:(g + 1) * hp] * tail_f[:, g * hp:(g + 1) * hp]).astype(BF16)
        st_ref[g] = st * dlast_f[:, g * hp:(g + 1) * hp] + _dot(bm.T.astype(BF16), xsc)

    y = jnp.concatenate(y_parts, axis=1) + dfull_ref[...] * xs
    y = y * _silu(z_ref[...].astype(F32))
    outs = []
    for g in range(SSD_GROUPS):
        yg = y[:, g * hp:(g + 1) * hp]
        ms = jnp.mean(yg * yg, axis=-1, keepdims=True)
        outs.append(yg * lax.rsqrt(ms + EPS))
    y_ref[...] = (jnp.concatenate(outs, axis=1) * nw_ref[...]).astype(y_ref.dtype)

    @pl.when(last)
    def _():
        for pr in range(npair):
            g, j = divmod(pr, npair // SSD_GROUPS)
            sn_ref[0, pr] = st_ref[g, :, j * LANE:(j + 1) * LANE].T


def ssd_scan(xbc, small, z, s0, c0, p, row0, bsz, seqlen, q, y_prev=None):
    nc = seqlen // q
    blk0 = row0 // q
    assert row0 % q == 0 and seqlen % q == 0
    npair = SSD_HEADS // 2
    tok = lambda b, c: (blk0 + b * nc + c, 0)
    fixed2 = lambda b, c: (0, 0)
    t = xbc.shape[0]
    in_specs = [
        pl.BlockSpec((q, SSD_CONV_DIM), tok),
        pl.BlockSpec((q, SMALL_W), tok),
        pl.BlockSpec((q, SSD_INNER), tok),
        pl.BlockSpec((1, npair, LANE, SSD_STATE), lambda b, c: (b, 0, 0, 0)),
        pl.BlockSpec((1, CONV_TAIL, SSD_CONV_DIM), lambda b, c: (b, 0, 0)),
        pl.BlockSpec((CONV_W, SSD_CONV_DIM), fixed2),
        pl.BlockSpec((1, SSD_CONV_DIM), fixed2),
        pl.BlockSpec((1, SSD_HEADS), fixed2),
        pl.BlockSpec((SSD_HEADS, 1), fixed2),
        pl.BlockSpec((1, SSD_HEADS), fixed2),
        pl.BlockSpec((SSD_HEADS, 1), fixed2),
        pl.BlockSpec((1, SSD_INNER), fixed2),
        pl.BlockSpec((1, SSD_INNER), fixed2),
        pl.BlockSpec((SSD_HEADS, SSD_INNER), fixed2),
    ]
    args = [xbc, small, z, s0.reshape(bsz, npair, LANE, SSD_STATE), c0,
            p['ssd_conv_w'], p['ssd_conv_b'], p['ssd_dt_bias'], p['ssd_dt_bias'].reshape(-1, 1),
            p['ssd_a_log'], p['ssd_a_log'].reshape(-1, 1), p['ssd_d_full'], p['ssd_norm'], p['ssd_expand']]
    aliases = {}
    if y_prev is not None:
        in_specs.append(pl.BlockSpec(memory_space=pl.ANY))
        args.append(y_prev)
        aliases = {len(args) - 1: 0}

    def body(*refs):
        if y_prev is not None:
            refs = refs[:14] + refs[15:]
        _ssd_kernel(*refs, q=q, nc=nc)

    y, sn, cn = pl.pallas_call(
        body,
        grid=(bsz, nc),
        in_specs=in_specs,
        out_specs=[pl.BlockSpec((q, SSD_INNER), tok),
                   pl.BlockSpec((1, npair, LANE, SSD_STATE), lambda b, c: (b, 0, 0, 0)),
                   pl.BlockSpec((1, CONV_TAIL, SSD_CONV_DIM), lambda b, c: (b, 0, 0))],
        out_shape=[jax.ShapeDtypeStruct((t, SSD_INNER), BF16),
                   jax.ShapeDtypeStruct((bsz, npair, LANE, SSD_STATE), F32),
                   jax.ShapeDtypeStruct((bsz, CONV_TAIL, SSD_CONV_DIM), F32)],
        scratch_shapes=[pltpu.VMEM((q + SUBLANE, SSD_CONV_DIM), F32),
                        pltpu.VMEM((SSD_GROUPS, SSD_STATE, SSD_INNER // SSD_GROUPS), F32)],
        input_output_aliases=aliases,
        compiler_params=_cparams("parallel", "arbitrary"),
        name="ssd_scan",
    )(*args)
    return y, sn.reshape(bsz, SSD_HEADS, SSD_HEAD_DIM, SSD_STATE), cn


def _solve_unit_lower(amats, rhss, q):
    ps = [-a for a in amats]
    ys = list(rhss)
    span = 1
    while span < q:
        last = 2 * span >= q
        nxt_p, nxt_y = [], []
        for p, y in zip(ps, ys):
            ph, plo = _split(p)
            yh, ylo = _split(y)
            if last:
                rh, rl = yh, ylo
            else:
                rh = jnp.concatenate([yh, ph], axis=1)
                rl = jnp.concatenate([ylo, plo], axis=1)
            z = _dot(ph, rh) + (_dot(ph, rl) + _dot(plo, rh))
            w = y.shape[1]
            nxt_y.append(y + z[:, :w])
            nxt_p.append(None if last else z[:, w:])
        ps, ys = nxt_p, nxt_y
        span *= 2
    return ys


def _gdn_kernel(qkv_ref, sm_ref, z_ref, s0_ref, c0_ref, cw_ref, dtb_ref, dtbt_ref,
                alog_ref, alogt_ref, nw_ref,
                o_ref, sn_ref, cn_ref, ext_ref, st_ref, *, q, nc):
    c = pl.program_id(1)
    first = c == 0
    last = c == nc - 1
    rep = GDN_V_HEADS // GDN_K_HEADS
    b_off = SSD_HEADS
    a_off = SSD_HEADS + GDN_V_HEADS

    @pl.when(first)
    def _():
        st_ref[...] = s0_ref[0]

    qkv = _silu(_conv_chunk(qkv_ref, c0_ref, cn_ref, ext_ref, cw_ref, first, last, q))
    sm = sm_ref[...]
    sm_t = sm.T
    beta = jax.nn.sigmoid(sm[:, b_off:b_off + GDN_V_HEADS])
    g = -jnp.exp(alog_ref[...]) * _softplus(sm[:, a_off:a_off + GDN_V_HEADS] + dtb_ref[...])
    g_t = -jnp.exp(alogt_ref[...]) * _softplus(sm_t[a_off:a_off + GDN_V_HEADS, :] + dtbt_ref[...])
    row = _iota2((q, q), 0)
    col = _iota2((q, q), 1)
    causal = row >= col
    strict = row > col
    low = jnp.where(causal, 1.0, 0.0).astype(BF16)
    upp = jnp.where(row <= col, 1.0, 0.0).astype(BF16)
    gam = _dot2_exact_lhs(low, g)
    gh, gl = _split(g_t)
    gam_t = _dot(gh, upp) + _dot(gl, upp)
    gam_last = gam[q - 1:q, :]
    eg = jnp.exp(gam)
    tail = jnp.exp(gam_last - gam)
    elast = jnp.exp(gam_last)

    qk_l, kk_l, kn_l, qk_rows = [], [], [], []
    for kh in range(GDN_K_HEADS):
        qh = qkv[:, kh * GDN_K_DIM:(kh + 1) * GDN_K_DIM]
        kx = qkv[:, GDN_QK + kh * GDN_K_DIM:GDN_QK + (kh + 1) * GDN_K_DIM]
        qn = qh * lax.rsqrt(jnp.sum(qh * qh, axis=-1, keepdims=True) + EPS) * (GDN_K_DIM ** -0.5)
        kn = kx * lax.rsqrt(jnp.sum(kx * kx, axis=-1, keepdims=True) + EPS)
        both = jnp.concatenate([qn.astype(BF16), kn.astype(BF16)], axis=0)
        prod = _dot_nt(both, both[q:])
        qk_l.append(prod[:q])
        kk_l.append(prod[q:])
        kn_l.append(kn)
        qk_rows.append(both)
    s_l = [st_ref[h] for h in range(GDN_V_HEADS)]
    qs_ks = [_dot(qk_rows[h // rep], s_l[h].astype(BF16)) for h in range(GDN_V_HEADS)]
    amats, rhss, decs = [], [], []
    for h in range(GDN_V_HEADS):
        v = qkv[:, 2 * GDN_QK + h * GDN_V_DIM:2 * GDN_QK + (h + 1) * GDN_V_DIM]
        diff = gam[:, h:h + 1] - gam_t[h:h + 1, :]
        dec = jnp.exp(jnp.where(causal, diff, -jnp.inf))
        bh = beta[:, h:h + 1]
        amats.append(jnp.where(strict, bh * dec * kk_l[h // rep], 0.0))
        rhss.append(bh * (v - eg[:, h:h + 1] * qs_ks[h][q:]))
        decs.append(dec)
    us = _solve_unit_lower(amats, rhss, q)
    z = z_ref[...].astype(F32)
    outs = []
    for h in range(GDN_V_HEADS):
        u_b = us[h].astype(BF16)
        o = _dot((qk_l[h // rep] * decs[h]).astype(BF16), u_b) + eg[:, h:h + 1] * qs_ks[h][:q]
        kt = (kn_l[h // rep] * tail[:, h:h + 1]).T.astype(BF16)
        st_ref[h] = s_l[h] * elast[:, h:h + 1] + _dot(kt, u_b)
        ms = jnp.mean(o * o, axis=-1, keepdims=True)
        zh = z[:, h * GDN_V_DIM:(h + 1) * GDN_V_DIM]
        outs.append(o * lax.rsqrt(ms + EPS) * nw_ref[...] * _silu(zh))
    o_ref[...] = jnp.concatenate(outs, axis=1).astype(o_ref.dtype)

    @pl.when(last)
    def _():
        sn_ref[0] = st_ref[...]


def gdn_scan(qkv, small, z, s0, c0, p, row0, bsz, seqlen, q, o_prev=None):
    nc = seqlen // q
    blk0 = row0 // q
    assert row0 % q == 0 and seqlen % q == 0
    tok = lambda b, c: (blk0 + b * nc + c, 0)
    fixed2 = lambda b, c: (0, 0)
    t = qkv.shape[0]
    st_spec = pl.BlockSpec((1, GDN_V_HEADS, GDN_K_DIM, GDN_V_DIM), lambda b, c: (b, 0, 0, 0))
    cv_spec = pl.BlockSpec((1, CONV_TAIL, GDN_CONV_DIM), lambda b, c: (b, 0, 0))
    in_specs = [
        pl.BlockSpec((q, GDN_CONV_DIM), tok),
        pl.BlockSpec((q, SMALL_W), tok),
        pl.BlockSpec((q, GDN_VD), tok),
        st_spec, cv_spec,
        pl.BlockSpec((CONV_W, GDN_CONV_DIM), fixed2),
        pl.BlockSpec((1, GDN_V_HEADS), fixed2),
        pl.BlockSpec((GDN_V_HEADS, 1), fixed2),
        pl.BlockSpec((1, GDN_V_HEADS), fixed2),
        pl.BlockSpec((GDN_V_HEADS, 1), fixed2),
        pl.BlockSpec((1, GDN_V_DIM), fixed2),
    ]
    args = [qkv, small, z, s0, c0, p['gdn_conv_w'], p['gdn_dt_bias'], p['gdn_dt_bias'].reshape(-1, 1),
            p['gdn_a_log'], p['gdn_a_log'].reshape(-1, 1), p['gdn_norm']]
    aliases = {}
    if o_prev is not None:
        in_specs.append(pl.BlockSpec(memory_space=pl.ANY))
        args.append(o_prev)
        aliases = {len(args) - 1: 0}

    def body(*refs):
        if o_prev is not None:
            refs = refs[:11] + refs[12:]
        _gdn_kernel(*refs, q=q, nc=nc)

    return pl.pallas_call(
        body,
        grid=(bsz, nc),
        in_specs=in_specs,
        out_specs=[pl.BlockSpec((q, GDN_VD), tok), st_spec, cv_spec],
        out_shape=[jax.ShapeDtypeStruct((t, GDN_VD), BF16),
                   jax.ShapeDtypeStruct((bsz, GDN_V_HEADS, GDN_K_DIM, GDN_V_DIM), F32),
                   jax.ShapeDtypeStruct((bsz, CONV_TAIL, GDN_CONV_DIM), F32)],
        scratch_shapes=[pltpu.VMEM((q + SUBLANE, GDN_CONV_DIM), F32),
                        pltpu.VMEM((GDN_V_HEADS, GDN_K_DIM, GDN_V_DIM), F32)],
        input_output_aliases=aliases,
        compiler_params=_cparams("parallel", "arbitrary"),
        name="gdn_scan",
    )(*args)


def _hgrn_kernel(q_ref, f_ref, i_ref, gz_ref, s0_ref, lb_ref, nw_ref,
                 o_ref, sn_ref, st_ref, *, q, nc):
    c = pl.program_id(1)
    nsub = q // HGRN_SUB

    @pl.when(c == 0)
    def _():
        for h in range(HGRN_HEADS):
            st_ref[h] = s0_ref[0, h].T

    lb = lb_ref[...]
    f = lb + (1.0 - lb) * jax.nn.sigmoid(f_ref[...])
    logf = jnp.log(f)
    kk = 1.0 - f
    qs = _silu(q_ref[...].astype(F32))
    row = _iota2((q, q), 0)
    col = _iota2((q, q), 1)
    same = (row // HGRN_SUB) == (col // HGRN_SUB)
    intra = same & (row >= col)
    m_pref = jnp.where(intra, 1.0, 0.0).astype(BF16)
    m_blk = jnp.where(same, 1.0, 0.0).astype(BF16)
    lh, ll = _split(logf)
    gam = _dot(m_pref, lh) + _dot(m_pref, ll)
    gtot = _dot(m_blk, lh) + _dot(m_blk, ll)
    qt = qs * jnp.exp(gam)
    kt = kk * jnp.exp(jnp.minimum(-gam, EXP_CLAMP))
    kd = kk * jnp.exp(gtot - gam)
    edec = jnp.exp(gtot)
    vv = i_ref[...].astype(F32)
    gz = gz_ref[...].astype(F32)
    rsub = _iota2((q, 1), 0) // HGRN_SUB

    heads = range(HGRN_HEADS)
    sls = [slice(h * HGRN_K_DIM, (h + 1) * HGRN_K_DIM) for h in heads]
    qt_b = [qt[:, sl].astype(BF16) for sl in sls]
    v_t = [vv[:, sl].T.astype(BF16) for sl in sls]
    o_in = []
    for h in heads:
        att = jnp.where(intra, _dot_nt(qt_b[h], kt[:, sls[h]].astype(BF16)), 0.0)
        o_in.append(_dot(att.astype(BF16), vv[:, sls[h]].astype(BF16)))

    def window_update(j):
        return [_dot(v_t[h], jnp.where(rsub == j, kd[:, sls[h]], 0.0).astype(BF16)) for h in heads]

    s_t = [st_ref[h] for h in heads]
    o_x = [[] for _ in heads]
    upd_next = window_update(0)
    for j in range(nsub):
        upd = upd_next
        if j + 1 < nsub:
            upd_next = window_update(j + 1)
        r0 = j * HGRN_SUB
        for h in heads:
            o_x[h].append(_dot_nt(qt_b[h][r0:r0 + HGRN_SUB, :], s_t[h].astype(BF16)))
            s_t[h] = s_t[h] * edec[r0:r0 + 1, sls[h]] + upd[h]
    outs = []
    for h in heads:
        st_ref[h] = s_t[h]
        o = o_in[h] + jnp.concatenate(o_x[h], axis=0)
        ms = jnp.mean(o * o, axis=-1, keepdims=True)
        outs.append(o * lax.rsqrt(ms + EPS) * nw_ref[...] * _silu(gz[:, sls[h]]))
    o_ref[...] = jnp.concatenate(outs, axis=1).astype(o_ref.dtype)

    @pl.when(c == nc - 1)
    def _():
        for h in range(HGRN_HEADS):
            sn_ref[0, h] = st_ref[h].T


def hgrn_scan(qp, fz, iv, gz, s0, lb, nw, row0, bsz, seqlen, q, o_prev=None):
    nc = seqlen // q
    blk0 = row0 // q
    assert row0 % q == 0 and seqlen % q == 0 and q % HGRN_SUB == 0
    hk = HGRN_HEADS * HGRN_K_DIM
    hv = HGRN_HEADS * HGRN_V_DIM
    tok = lambda b, c: (blk0 + b * nc + c, 0)
    fixed2 = lambda b, c: (0, 0)
    t = qp.shape[0]
    st_spec = pl.BlockSpec((1, HGRN_HEADS, HGRN_K_DIM, HGRN_V_DIM), lambda b, c: (b, 0, 0, 0))
    in_specs = [pl.BlockSpec((q, hk), tok), pl.BlockSpec((q, hk), tok),
                pl.BlockSpec((q, hv), tok), pl.BlockSpec((q, hv), tok),
                st_spec, pl.BlockSpec((1, hk), fixed2), pl.BlockSpec((1, HGRN_V_DIM), fixed2)]
    args = [qp, fz, iv, gz, s0, lb, nw]
    aliases = {}
    if o_prev is not None:
        in_specs.append(pl.BlockSpec(memory_space=pl.ANY))
        args.append(o_prev)
        aliases = {len(args) - 1: 0}

    def body(*refs):
        if o_prev is not None:
            refs = refs[:7] + refs[8:]
        _hgrn_kernel(*refs, q=q, nc=nc)

    return pl.pallas_call(
        body,
        grid=(bsz, nc),
        in_specs=in_specs,
        out_specs=[pl.BlockSpec((q, hv), tok), st_spec],
        out_shape=[jax.ShapeDtypeStruct((t, hv), BF16),
                   jax.ShapeDtypeStruct((bsz, HGRN_HEADS, HGRN_K_DIM, HGRN_V_DIM), F32)],
        scratch_shapes=[pltpu.VMEM((HGRN_HEADS, HGRN_V_DIM, HGRN_K_DIM), F32)],
        input_output_aliases=aliases,
        compiler_params=_cparams("parallel", "arbitrary"),
        name="hgrn_scan",
    )(*args)


def _out_proj_kernel(*refs, n_in):
    a_refs = refs[:n_in]
    w_refs = refs[n_in:2 * n_in]
    h_ref, o_ref = refs[2 * n_in], refs[2 * n_in + 1]
    acc = h_ref[...]
    for a_ref, w_ref in zip(a_refs, w_refs):
        acc = acc + _dot(a_ref[...], w_ref[...])
    o_ref[...] = acc


def out_proj(acts, ws, h, tm):
    t, d = h.shape
    n_in = len(acts)
    return pl.pallas_call(
        functools.partial(_out_proj_kernel, n_in=n_in),
        grid=(t // tm,),
        in_specs=([pl.BlockSpec((tm, a.shape[1]), lambda i: (i, 0)) for a in acts]
                  + [pl.BlockSpec(w.shape, lambda i: (0, 0)) for w in ws]
                  + [pl.BlockSpec((tm, d), lambda i: (i, 0))]),
        out_specs=pl.BlockSpec((tm, d), lambda i: (i, 0)),
        out_shape=jax.ShapeDtypeStruct((t, d), F32),
        compiler_params=_cparams("parallel"),
        name="out_proj",
    )(*acts, *ws, h)


R_E0, R_E1, R_RANK0, R_RANK1, R_GATE0, R_GATE1 = range(6)
ROUTE_ROWS = 8
ROUTER_ROWS = SUBLANE + MOE_EXPERTS


def _router_kernel(h_ref, g_ref, w_ref, xn_ref, info_ref, cnt_ref, carry_ref, *, tm, nt):
    i = pl.program_id(0)

    @pl.when(i == 0)
    def _():
        carry_ref[...] = jnp.zeros_like(carry_ref)

    x = h_ref[...]
    ms = jnp.mean(x * x, axis=-1, keepdims=True)
    xn = x * lax.rsqrt(ms + EPS) * g_ref[...]
    xn_ref[...] = xn
    logit = _dot3_nt(w_ref[...], xn)
    lg = logit[:MOE_GROUPS, :]
    gmax = jnp.max(lg, axis=0, keepdims=True)
    gidx = _iota2(lg.shape, 0)
    g_sel = jnp.min(jnp.where(lg == gmax, gidx, MOE_GROUPS), axis=0, keepdims=True)
    g_prob = 1.0 / jnp.sum(jnp.exp(lg - gmax), axis=0, keepdims=True)
    el = jnp.zeros((MOE_EPG, tm), F32)
    for g in range(MOE_GROUPS):
        el = jnp.where(g_sel == g, logit[SUBLANE + g * MOE_EPG:SUBLANE + (g + 1) * MOE_EPG, :], el)
    eidx = _iota2(el.shape, 0)
    m1 = jnp.max(el, axis=0, keepdims=True)
    i1 = jnp.min(jnp.where(el == m1, eidx, MOE_EPG), axis=0, keepdims=True)
    el2 = jnp.where(eidx == i1, -jnp.inf, el)
    m2 = jnp.max(el2, axis=0, keepdims=True)
    i2 = jnp.min(jnp.where(el2 == m2, eidx, MOE_EPG), axis=0, keepdims=True)
    ex = jnp.exp(m2 - m1)
    gate0 = g_prob / (1.0 + ex)
    gate1 = g_prob * ex / (1.0 + ex)
    e0 = g_sel * MOE_EPG + i1
    e1 = g_sel * MOE_EPG + i2

    xid = _iota2((MOE_EXPERTS, tm), 0)
    oh0 = xid == e0
    oh1 = xid == e1
    onehot = jnp.where(oh0 | oh1, 1.0, 0.0).astype(BF16)
    upp = jnp.where(_iota2((tm, tm), 0) <= _iota2((tm, tm), 1), 1.0, 0.0).astype(BF16)
    pref = _dot(onehot, upp)
    base = carry_ref[:, 0:1] + pref - 1.0
    rank0 = jnp.sum(jnp.where(oh0, base, 0.0), axis=0, keepdims=True)
    rank1 = jnp.sum(jnp.where(oh1, base, 0.0), axis=0, keepdims=True)
    new_carry = carry_ref[...] + pref[:, tm - 1:tm]
    carry_ref[...] = new_carry

    info_ref[R_E0:R_E0 + 1, :] = e0.astype(F32)
    info_ref[R_E1:R_E1 + 1, :] = e1.astype(F32)
    info_ref[R_RANK0:R_RANK0 + 1, :] = rank0
    info_ref[R_RANK1:R_RANK1 + 1, :] = rank1
    info_ref[R_GATE0:R_GATE0 + 1, :] = gate0
    info_ref[R_GATE1:R_GATE1 + 1, :] = gate1
    info_ref[R_GATE1 + 1:ROUTE_ROWS, :] = jnp.zeros((ROUTE_ROWS - R_GATE1 - 1, tm), F32)

    @pl.when(i == nt - 1)
    def _():
        cnt_ref[...] = new_carry


def moe_router(h, gain, w_rt, tm):
    t, d = h.shape
    nt = t // tm
    return pl.pallas_call(
        functools.partial(_router_kernel, tm=tm, nt=nt),
        grid=(nt,),
        in_specs=[pl.BlockSpec((tm, d), lambda i: (i, 0)),
                  pl.BlockSpec((1, d), lambda i: (0, 0)),
                  pl.BlockSpec((ROUTER_ROWS, d), lambda i: (0, 0))],
        out_specs=[pl.BlockSpec((tm, d), lambda i: (i, 0)),
                   pl.BlockSpec((ROUTE_ROWS, tm), lambda i: (0, i)),
                   pl.BlockSpec((MOE_EXPERTS, LANE), lambda i: (0, 0))],
        out_shape=[jax.ShapeDtypeStruct((t, d), F32),
                   jax.ShapeDtypeStruct((ROUTE_ROWS, t), F32),
                   jax.ShapeDtypeStruct((MOE_EXPERTS, LANE), F32)],
        scratch_shapes=[pltpu.VMEM((MOE_EXPERTS, LANE), F32)],
        compiler_params=_cparams("arbitrary"),
        name="moe_router",
    )(h, gain.reshape(1, d), w_rt)


def _row_dma_loops(row_copy, tm):
    def each(fn):
        def body(g, carry):
            base = pl.multiple_of(g * SUBLANE, SUBLANE)
            for r in range(SUBLANE):
                fn(base + r)
            return carry
        return lambda: lax.fori_loop(0, tm // SUBLANE, body, 0)

    def start(row):
        row_copy(row, 0).start(priority=0)
        row_copy(row, 1).start(priority=1)

    def wait(row):
        row_copy(row, 0).wait()
        row_copy(row, 1).wait()

    return each(start), each(wait)


def _dispatch_kernel(pad_end_ref, own_end_ref, dest_ref, xn_ref, *rest, tm, first):
    buf_ref, zero_ref, sem, zsem = rest[-4:]

    def zero_padding_tiles():
        zero_ref[...] = jnp.zeros_like(zero_ref)

        def tile_copy(j):
            return pltpu.make_async_copy(
                zero_ref, buf_ref.at[pl.ds(pl.multiple_of(j * MOE_ROW_TILE, MOE_ROW_TILE), MOE_ROW_TILE)], zsem)

        spans = [(own_end_ref[e] // MOE_ROW_TILE, pad_end_ref[e] // MOE_ROW_TILE) for e in range(MOE_EXPERTS)]
        spans.append((pad_end_ref[MOE_EXPERTS - 1] // MOE_ROW_TILE, buf_ref.shape[0] // MOE_ROW_TILE))
        for lo, hi in spans:
            lax.fori_loop(lo, hi, lambda j, c: (tile_copy(j).start(), c)[1], 0)
        for lo, hi in spans:
            lax.fori_loop(lo, hi, lambda j, c: (tile_copy(j).wait(), c)[1], 0)

    if first:
        pl.when(pl.program_id(0) == 0)(zero_padding_tiles)

    def row_copy(row, k):
        return pltpu.make_async_copy(xn_ref.at[pl.ds(row, 1)],
                                     buf_ref.at[pl.ds(dest_ref[0, 0, k * tm + row], 1)], sem)

    start_all, wait_all = _row_dma_loops(row_copy, tm)
    start_all()
    wait_all()


def moe_dispatch(xn, dest3, pad_end, own_end, n_rows, tm, buf_prev=None):
    t, d = xn.shape
    in_specs = [pl.BlockSpec((1, 1, 2 * tm), lambda i, pe, pd: (i, 0, 0), memory_space=pltpu.SMEM),
                pl.BlockSpec((tm, d), lambda i, pe, pd: (i, 0))]
    args = [pad_end, own_end, dest3, xn]
    aliases = {}
    if buf_prev is not None:
        in_specs.append(pl.BlockSpec(memory_space=pl.ANY))
        args.append(buf_prev)
        aliases = {len(args) - 1: 0}
    return pl.pallas_call(
        functools.partial(_dispatch_kernel, tm=tm, first=buf_prev is None),
        grid_spec=pltpu.PrefetchScalarGridSpec(
            num_scalar_prefetch=2,
            grid=(t // tm,),
            in_specs=in_specs,
            out_specs=pl.BlockSpec(memory_space=pl.ANY),
            scratch_shapes=[pltpu.VMEM((MOE_ROW_TILE, d), xn.dtype),
                            pltpu.SemaphoreType.DMA(()), pltpu.SemaphoreType.DMA(())],
        ),
        out_shape=jax.ShapeDtypeStruct((n_rows, d), xn.dtype),
        input_output_aliases=aliases,
        compiler_params=_cparams("arbitrary"),
        name="moe_dispatch",
    )(*args)


def _expert_kernel(te_ref, nu_ref, x_ref, wg_ref, wu_ref, wd_ref, y_ref, wg_b, wu_b, wd_b):
    i = pl.program_id(0)

    @pl.when((i == 0) | (te_ref[i] != te_ref[jnp.maximum(i - 1, 0)]))
    def _():
        wg_b[...] = wg_ref[0].astype(BF16)
        wu_b[...] = wu_ref[0].astype(BF16)
        wd_b[...] = wd_ref[0].astype(BF16)

    @pl.when(i < nu_ref[0])
    def _():
        x = x_ref[...].astype(BF16)
        hid = _silu(_dot(x, wg_b[...])) * _dot(x, wu_b[...])
        y_ref[...] = _dot(hid.astype(BF16), wd_b[...])

    @pl.when(i >= nu_ref[0])
    def _():
        y_ref[...] = jnp.zeros_like(y_ref)


def moe_experts(buf, tile_e, n_used, wg, wu, wd, first_expert):
    n_rows, d = buf.shape
    n_tiles = n_rows // MOE_ROW_TILE
    return pl.pallas_call(
        _expert_kernel,
        grid_spec=pltpu.PrefetchScalarGridSpec(
            num_scalar_prefetch=2,
            grid=(n_tiles,),
            in_specs=[pl.BlockSpec((MOE_ROW_TILE, d), lambda i, te, nu: (jnp.minimum(i, nu[0] - 1), 0)),
                      pl.BlockSpec((1, d, MOE_D_EXPERT), lambda i, te, nu: (first_expert + te[i], 0, 0)),
                      pl.BlockSpec((1, d, MOE_D_EXPERT), lambda i, te, nu: (first_expert + te[i], 0, 0)),
                      pl.BlockSpec((1, MOE_D_EXPERT, d), lambda i, te, nu: (first_expert + te[i], 0, 0))],
            out_specs=pl.BlockSpec((MOE_ROW_TILE, d), lambda i, te, nu: (i, 0)),
            scratch_shapes=[pltpu.VMEM((d, MOE_D_EXPERT), BF16), pltpu.VMEM((d, MOE_D_EXPERT), BF16),
                            pltpu.VMEM((MOE_D_EXPERT, d), BF16)],
        ),
        out_shape=jax.ShapeDtypeStruct((n_rows, d), F32),
        compiler_params=_cparams("arbitrary"),
        name="moe_experts",
    )(tile_e, n_used, buf, wg, wu, wd)


def _combine_kernel(dest_ref, h_ref, info_ref, fg_ref, yb_ref, o_ref, ybuf, sem, *, tm, final_norm):
    def row_copy(row, k):
        return pltpu.make_async_copy(yb_ref.at[pl.ds(dest_ref[0, 0, k * tm + row], 1)],
                                     ybuf.at[k, pl.ds(row, 1)], sem)

    start_all, wait_all = _row_dma_loops(row_copy, tm)
    start_all()
    info_t = info_ref[...].T
    g0 = info_t[:, R_GATE0:R_GATE0 + 1]
    g1 = info_t[:, R_GATE1:R_GATE1 + 1]
    wait_all()
    y = h_ref[...] + (g0 * ybuf[0] + g1 * ybuf[1])
    if final_norm:
        ms = jnp.mean(y * y, axis=-1, keepdims=True)
        y = y * lax.rsqrt(ms + EPS) * fg_ref[...]
    o_ref[...] = y


def moe_combine(h, info, dest3, yb, final_gain, final_norm, tm):
    t, d = h.shape
    return pl.pallas_call(
        functools.partial(_combine_kernel, tm=tm, final_norm=final_norm),
        grid=(t // tm,),
        in_specs=[pl.BlockSpec((1, 1, 2 * tm), lambda i: (i, 0, 0), memory_space=pltpu.SMEM),
                  pl.BlockSpec((tm, d), lambda i: (i, 0)),
                  pl.BlockSpec((ROUTE_ROWS, tm), lambda i: (0, i)),
                  pl.BlockSpec((1, d), lambda i: (0, 0)),
                  pl.BlockSpec(memory_space=pl.ANY)],
        out_specs=pl.BlockSpec((tm, d), lambda i: (i, 0)),
        out_shape=jax.ShapeDtypeStruct((t, d), F32),
        scratch_shapes=[pltpu.VMEM((2, tm, d), F32), pltpu.SemaphoreType.DMA(())],
        compiler_params=_cparams("arbitrary"),
        name="moe_combine",
    )(dest3, h, info, final_gain.reshape(1, d), yb)


def hier_moe_residual(hs, tms, gain, w_rt, wg, wu, wd, first_expert, final_gain, final_norm):
    d = hs[0].shape[1]
    routed = [moe_router(h, gain, w_rt, tm) for h, tm in zip(hs, tms)]
    counts = [cnt[:, 0].astype(jnp.int32) for _, _, cnt in routed]
    total = sum(counts)
    padded = (total + MOE_ROW_TILE - 1) // MOE_ROW_TILE * MOE_ROW_TILE
    pad_end = jnp.cumsum(padded).astype(jnp.int32)
    pad_start = pad_end - padded
    n_tiles = 2 * sum(h.shape[0] for h in hs) // MOE_ROW_TILE + MOE_EXPERTS
    n_used = (pad_end[-1] // MOE_ROW_TILE).astype(jnp.int32)
    tile_row = jnp.minimum(jnp.arange(n_tiles, dtype=jnp.int32), n_used - 1) * MOE_ROW_TILE
    tile_e = jnp.sum(tile_row[:, None] >= pad_end[None, :], axis=1).astype(jnp.int32)
    buf = None
    dests = []
    first_row = pad_start
    for (xn, info, _), cnt, tm in zip(routed, counts, tms):
        t = xn.shape[0]
        e01 = info[R_E0:R_E1 + 1].astype(jnp.int32)
        rank01 = info[R_RANK0:R_RANK1 + 1].astype(jnp.int32)
        onehot = e01[:, :, None] == jnp.arange(MOE_EXPERTS, dtype=jnp.int32)
        dest = rank01 + jnp.sum(jnp.where(onehot, first_row, 0), axis=-1)
        dest3 = dest.reshape(2, t // tm, tm).transpose(1, 0, 2).reshape(t // tm, 1, 2 * tm)
        buf = moe_dispatch(xn, dest3, pad_end, first_row + cnt, n_tiles * MOE_ROW_TILE, tm, buf)
        dests.append(dest3)
        first_row = first_row + cnt
    yb = moe_experts(buf, tile_e, n_used.reshape(1), wg, wu, wd, first_expert)
    return [moe_combine(h, info, dest3, yb, final_gain, final_norm, tm)
            for h, (_, info, _), dest3, tm in zip(hs, routed, dests, tms)]


def _cast_split_kernel(w_ref, *o_refs, pieces):
    w = w_ref[...]
    for o_ref, cols in zip(o_refs, pieces):
        parts = [w[:, a:b] for a, b in cols]
        width = sum(b - a for a, b in cols)
        if width < o_ref.shape[1]:
            parts.append(jnp.zeros((w.shape[0], o_ref.shape[1] - width), F32))
        o_ref[...] = (parts[0] if len(parts) == 1 else jnp.concatenate(parts, axis=1)).astype(BF16)


def cast_split(w, pieces, widths, tk, name):
    k, n = w.shape
    assert k % tk == 0
    return pl.pallas_call(
        functools.partial(_cast_split_kernel, pieces=pieces),
        grid=(k // tk,),
        in_specs=[pl.BlockSpec((tk, n), lambda i: (i, 0))],
        out_specs=[pl.BlockSpec((tk, wd), lambda i: (i, 0)) for wd in widths],
        out_shape=[jax.ShapeDtypeStruct((k, wd), BF16) for wd in widths],
        compiler_params=_cparams("parallel"),
        name=name,
    )(w)


def _prep_ab_weights(w_in_ab):
    o = [int(v) for v in np.cumsum([0, SSD_INNER, SSD_CONV_DIM, SSD_HEADS, GDN_CONV_DIM, GDN_VD,
                                    GDN_V_HEADS, GDN_V_HEADS])]
    seg = lambda k: (o[k], o[k + 1])
    pieces = ((seg(0),), (seg(1),), (seg(3),), (seg(4),), (seg(2), seg(5), seg(6)))
    widths = (SSD_INNER, SSD_CONV_DIM, GDN_CONV_DIM, GDN_VD, SMALL_W)
    return cast_split(w_in_ab, pieces, widths, 128, "prep_w_in_ab")


def _prep_c_weights(w_in_c):
    pieces = tuple(((k * D_MODEL, (k + 1) * D_MODEL),) for k in range(4))
    return cast_split(w_in_c, pieces, (D_MODEL,) * 4, 128, "prep_w_in_c")


def _ab_layer(h, states, w, g):
    st_ssd, st_ssd_conv, st_gdn, st_gdn_conv, _ = states
    zs, xbc, qkv, zg, small = norm_proj(h, w['norm_mix'][0], w['w_in_ab'], (BF16, BF16, BF16, BF16, F32),
                                        g['tm'], "in_proj_ab")
    y, n_ssd, n_ssd_conv = ssd_scan(xbc, small, zs, st_ssd[:, 0], st_ssd_conv[:, 0], w, 0,
                                    g['bsz'], g['seqlen'], g['q_ssd'])
    o, n_gdn, n_gdn_conv = gdn_scan(qkv, small, zg, st_gdn[:, 0], st_gdn_conv[:, 0], w, 0,
                                    g['bsz'], g['seqlen'], g['q_gdn'])
    h = out_proj([y, o], [w['w_out_ab'][:SSD_INNER], w['w_out_ab'][SSD_INNER:]], h, g['tm'])
    return h, (n_ssd[:, None], n_ssd_conv[:, None], n_gdn[:, None], n_gdn_conv[:, None])


def _c_layer(h, states, w, g):
    qp, fz, iv, gz = norm_proj(h, w['norm_mix'][1], w['w_in_c'], (BF16, F32, BF16, BF16), g['tm'], "in_proj_c")
    oc, n_hgrn = hgrn_scan(qp, fz, iv, gz, states[4][:, 0], w['hgrn_lb'], w['hgrn_norm'], 0,
                           g['bsz'], g['seqlen'], g['q_hgrn'])
    return out_proj([oc], [w['w_out_c']], h, g['tm']), n_hgrn[:, None]


def _moe_layer(hs, groups, w, li, final_norm):
    return hier_moe_residual(hs, [g['tm'] for g in groups], w['norm_ffn'][li], w['moe_w_rt'][li],
                             w['moe_w_gate'], w['moe_w_up'], w['moe_w_down'], li * MOE_EXPERTS,
                             w['norm_final'], final_norm)


def kernel(x_prompt, x_sample, state_ssd, state_ssd_conv, state_gdn, state_gdn_conv, state_hgrn,
           norm_mix, norm_ffn, norm_final,
           w_in_ab, ssd_conv_w, ssd_conv_b, ssd_dt_bias, ssd_a_log, ssd_d, ssd_norm,
           gdn_conv_w, gdn_dt_bias, gdn_a_log, gdn_norm, w_out_ab,
           w_in_c, hgrn_lb_logits, hgrn_norm, w_out_c,
           moe_w_group, moe_w_router, moe_w_gate, moe_w_up, moe_w_down):
    depth = norm_mix.shape[0]
    sm = jax.nn.softmax(hgrn_lb_logits.astype(F32), axis=0)
    lower_bounds = jnp.cumsum(sm, axis=0) - sm[0]
    w_rt = jnp.concatenate([jnp.swapaxes(moe_w_group, 1, 2),
                            jnp.zeros((depth, SUBLANE - MOE_GROUPS, D_MODEL), F32),
                            jnp.swapaxes(moe_w_router, 1, 2)], axis=1)
    expand = (jnp.arange(SSD_INNER)[None, :] // SSD_HEAD_DIM == jnp.arange(SSD_HEADS)[:, None]).astype(F32)
    w = dict(
        norm_mix=norm_mix, norm_ffn=norm_ffn, norm_final=norm_final,
        w_in_ab=_prep_ab_weights(w_in_ab[0]),
        ssd_conv_w=ssd_conv_w[0], ssd_conv_b=ssd_conv_b[0][None, :], ssd_dt_bias=ssd_dt_bias[0][None, :],
        ssd_a_log=ssd_a_log[0][None, :], ssd_d_full=jnp.repeat(ssd_d[0], SSD_HEAD_DIM)[None, :],
        ssd_norm=ssd_norm[0][None, :], ssd_expand=expand,
        gdn_conv_w=gdn_conv_w[0], gdn_dt_bias=gdn_dt_bias[0][None, :], gdn_a_log=gdn_a_log[0][None, :],
        gdn_norm=gdn_norm[0][None, :], w_out_ab=w_out_ab[0].astype(BF16),
        w_in_c=_prep_c_weights(w_in_c[0]),
        hgrn_lb=lower_bounds[1][None, :], hgrn_norm=hgrn_norm[0][None, :],
        w_out_c=w_out_c[0].astype(BF16), moe_w_rt=w_rt,
        moe_w_gate=moe_w_gate.reshape(-1, D_MODEL, MOE_D_EXPERT),
        moe_w_up=moe_w_up.reshape(-1, D_MODEL, MOE_D_EXPERT),
        moe_w_down=moe_w_down.reshape(-1, MOE_D_EXPERT, D_MODEL),
    )
    bp, lp, _ = x_prompt.shape
    bs, ls, _ = x_sample.shape
    zeros = lambda ref: jnp.zeros((bp,) + ref.shape[1:], x_prompt.dtype)
    p_states = tuple(zeros(s) for s in (state_ssd, state_ssd_conv, state_gdn, state_gdn_conv, state_hgrn))
    s_states = (state_ssd, state_ssd_conv, state_gdn, state_gdn_conv, state_hgrn)
    groups = [dict(bsz=bp, seqlen=lp, q_ssd=256, q_gdn=128, q_hgrn=128, tm=512),
              dict(bsz=bs, seqlen=ls, q_ssd=ls, q_gdn=ls, q_hgrn=ls, tm=bs * ls)]
    states = [p_states, s_states]
    hs = [x_prompt.reshape(bp * lp, D_MODEL), x_sample.reshape(bs * ls, D_MODEL)]
    ab = [_ab_layer(h, st, w, g) for h, st, g in zip(hs, states, groups)]
    hs = _moe_layer([a[0] for a in ab], groups, w, 0, False)
    cl = [_c_layer(h, st, w, g) for h, st, g in zip(hs, states, groups)]
    ys = _moe_layer([c[0] for c in cl], groups, w, 1, True)
    y_p, y_s = (y.reshape(g['bsz'], g['seqlen'], D_MODEL) for y, g in zip(ys, groups))
    return (y_p, y_s) + ab[0][1] + (cl[0][1],) + ab[1][1] + (cl[1][1],)
```
